```python
import math
import jax
import jax.numpy as jnp
from jax import lax
import numpy as np

D_MODEL = 1024
BATCH = 16
SEQ = 2048
DEPTH = 1
DEC_BATCH = 32
DEC_SEQ = 4
PAST_LEN = 16384
PAGE_SIZE = 128

N_META = 16
GLA_HEADS = 4
GLA_DK = D_MODEL // 2 // GLA_HEADS
GLA_DV = D_MODEL // GLA_HEADS
GLA_GATE_RANK = 16
GLA_GATE_TAU = 16.0
GLA_CHUNK = 64
HEAD_DIM = 64
DIFF_HEADS = D_MODEL // (2 * HEAD_DIM)
ROPE_THETA = 10000.0
Q_BLOCK = 128
D_FF = 256 * ((8 * D_MODEL // 3 + 255) // 256)
CONV_WIDTH = 3
NORM_EPS = 1e-5
DEEPNORM_ALPHA = (2.0 * DEPTH) ** 0.25
DEEPNORM_BETA = (8.0 * DEPTH) ** -0.25
SPLIT_SIZES = (
    GLA_HEADS * GLA_DK,
    GLA_HEADS * GLA_DK,
    GLA_HEADS * GLA_DV,
    GLA_HEADS * GLA_DV,
    GLA_GATE_RANK,
    2 * DIFF_HEADS * HEAD_DIM,
    2 * DIFF_HEADS * HEAD_DIM,
    DIFF_HEADS * 2 * HEAD_DIM,
    D_MODEL,
    D_MODEL,
)
D_IN = sum(SPLIT_SIZES)

kernel_name = 'gla_diffattn_convffn_hybrid_step'


def _split_points():
    pts, acc = [], 0
    for s in SPLIT_SIZES[:-1]:
        acc += s
        pts.append(acc)
    return pts


def layer_norm(x, g, b):
    xf = x.astype(jnp.float32)
    mu = jnp.mean(xf, -1, keepdims=True)
    var = jnp.mean(jnp.square(xf - mu), -1, keepdims=True)
    return ((xf - mu) * lax.rsqrt(var + NORM_EPS)).astype(x.dtype) * g + b


def rms_norm(x, g):
    xf = x.astype(jnp.float32)
    return (xf * lax.rsqrt(jnp.mean(jnp.square(xf), -1, keepdims=True) + NORM_EPS)).astype(x.dtype) * g


def rope(x, pos):
    d = x.shape[-1]
    inv_freq = 1.0 / (ROPE_THETA ** (jnp.arange(0, d, 2, dtype=jnp.float32) / d))
    ang = pos.astype(jnp.float32)[:, None] * inv_freq[None, :]
    ang = jnp.concatenate([ang, ang], axis=-1)[None, :, None, :]
    cos = jnp.cos(ang).astype(x.dtype)
    sin = jnp.sin(ang).astype(x.dtype)
    x1, x2 = jnp.split(x, 2, axis=-1)
    return x * cos + jnp.concatenate([-x2, x1], axis=-1) * sin


def project_inputs(h, pos, w_in, w_gate_up, b_gate):
    B, L, _ = h.shape
    z = h @ w_in
    gq, gk, gv, gr, ga, dq, dk, dv, ma, mb = jnp.split(z, _split_points(), axis=-1)
    gla_q = gq.reshape(B, L, GLA_HEADS, GLA_DK) * (GLA_DK ** -0.5)
    gla_k = gk.reshape(B, L, GLA_HEADS, GLA_DK)
    gla_v = gv.reshape(B, L, GLA_HEADS, GLA_DV)
    log_a = (jax.nn.log_sigmoid((ga @ w_gate_up + b_gate).astype(jnp.float32)) / GLA_GATE_TAU).reshape(B, L, GLA_HEADS, GLA_DK)
    diff_q = rope(dq.reshape(B, L, 2 * DIFF_HEADS, HEAD_DIM), pos) * (HEAD_DIM ** -0.5)
    diff_k = rope(dk.reshape(B, L, 2 * DIFF_HEADS, HEAD_DIM), pos)
    diff_v = dv.reshape(B, L, DIFF_HEADS, 2 * HEAD_DIM)
    return (gla_q, gla_k, gla_v, log_a, gr, diff_q, diff_k, diff_v, ma, mb)


def gla_chunked(q, k, v, log_a, s0, chunk):
    B, L, H, DK = q.shape
    DV = v.shape[-1]
    n = L // chunk
    f32 = jnp.float32

    def to_chunks(t):
        return jnp.moveaxis(t.astype(f32).reshape(B, n, chunk, H, t.shape[-1]), 1, 0)

    causal = jnp.tril(jnp.ones((chunk, chunk), dtype=bool))[None, :, :, None, None]

    def step(S, inp):
        qc, kc, vc, gc = inp
        bc = jnp.cumsum(gc, axis=1)
        o_inter = jnp.einsum('bchk,bhkv->bchv', qc * jnp.exp(bc), S)
        rel = bc[:, :, None] - bc[:, None, :]
        decay = jnp.exp(jnp.where(causal, rel, -jnp.inf))
        att = jnp.einsum('bihk,bjhk,bijhk->bhij', qc, kc, decay)
        o_intra = jnp.einsum('bhij,bjhv->bihv', att, vc)
        b_last = bc[:, -1]
        S_new = jnp.exp(b_last)[..., None] * S + jnp.einsum('bjhk,bjhv->bhkv', kc * jnp.exp(b_last[:, None] - bc), vc)
        return S_new, o_inter + o_intra

    S, o = lax.scan(step, s0.astype(f32), (to_chunks(q), to_chunks(k), to_chunks(v), to_chunks(log_a)))
    o = jnp.moveaxis(o, 0, 1).reshape(B, L, H, DV)
    return o.astype(v.dtype), S.astype(s0.dtype)


def diff_lambda(lq1, lk1, lq2, lk2, lam_init):
    f = lambda a: a.astype(jnp.float32)
    return jnp.exp(jnp.sum(f(lq1) * f(lk1))) - jnp.exp(jnp.sum(f(lq2) * f(lk2))) + lam_init


def diff_attn_prompt(q, k, v, lam):
    B, L = q.shape[:2]
    nb = -(-L // Q_BLOCK)
    Lp = nb * Q_BLOCK
    qh = jnp.pad(q.reshape(B, L, DIFF_HEADS, 2, HEAD_DIM), ((0, 0), (0, Lp - L), (0, 0), (0, 0), (0, 0)))
    qb = jnp.moveaxis(qh.reshape(B, nb, Q_BLOCK, DIFF_HEADS, 2, HEAD_DIM), 1, 0)
    kh = k.reshape(B, L, DIFF_HEADS, 2, HEAD_DIM)
    kpos = jnp.arange(L)

    def block(args):
        qblk, start = args
        s = jnp.einsum('bqhid,bkhid->bhiqk', qblk, kh).astype(jnp.float32)
        qpos = start + jnp.arange(Q_BLOCK)
        s = jnp.where(kpos[None, :] <= qpos[:, None], s, -jnp.inf)
        p = jax.nn.softmax(s, axis=-1)
        w = (p[:, :, 0] - lam * p[:, :, 1]).astype(v.dtype)
        return jnp.einsum('bhqk,bkhe->bqhe', w, v)

    o = lax.map(block, (qb, jnp.arange(nb) * Q_BLOCK))
    return jnp.moveaxis(o, 0, 1).reshape(B, Lp, DIFF_HEADS, 2 * HEAD_DIM)[:, :L]


def diff_attn_sample(q, k_new, v_new, cache_k, cache_v, page_table, layer, lam):
    Bd, T = q.shape[:2]
    f32 = jnp.float32
    qh = q.reshape(Bd, T, DIFF_HEADS, 2, HEAD_DIM)
    kn = k_new.reshape(Bd, T, DIFF_HEADS, 2, HEAD_DIM)
    s0 = jnp.einsum('bthid,bshid->bhits', qh, kn).astype(f32)
    s0 = jnp.where(jnp.tril(jnp.ones((T, T), dtype=bool)), s0, -jnp.inf)
    m0 = jnp.max(s0, axis=-1)
    p0 = jnp.exp(s0 - m0[..., None])
    l0 = jnp.sum(p0, axis=-1)
    acc0 = jnp.einsum('bhits,bshe->bhite', p0, v_new.astype(f32))

    def step(carry, pg):
        m, l, acc = carry
        phys = page_table[:, pg]
        kb = cache_k[layer, phys].reshape(Bd, PAGE_SIZE, DIFF_HEADS, 2, HEAD_DIM)
        vb = cache_v[layer, phys]
        s = jnp.einsum('bthid,bshid->bhits', qh, kb).astype(f32)
        m_new = jnp.maximum(m, jnp.max(s, axis=-1))
        corr = jnp.exp(m - m_new)
        pe = jnp.exp(s - m_new[..., None])
        acc = acc * corr[..., None] + jnp.einsum('bhits,bshe->bhite', pe, vb.astype(f32))
        return (m_new, l * corr + jnp.sum(pe, axis=-1), acc), None

    (m, l, acc), _ = lax.scan(step, (m0, l0, acc0), jnp.arange(page_table.shape[1]))
    o = acc / l[..., None]
    o = o[:, :, 0] - lam * o[:, :, 1]
    return jnp.moveaxis(o, 1, 2).astype(v_new.dtype)


def conv_ffn(h, p, prev):
    L = h.shape[1]
    u = h @ p['w_up'] + p['b_up']
    ext = jnp.concatenate([prev.astype(u.dtype), u], axis=1)
    c = p['conv_b'] + sum(ext[:, i:i + L] * p['conv_w'][i] for i in range(CONV_WIDTH))
    gate, up = jnp.split(c, 2, axis=-1)
    out = (jax.nn.gelu(gate) * up) @ p['w_down']
    return out, ext[:, L:]


def merge_and_ffn(h, o_gla, gr, o_diff, ma, mb, lam_init, p, conv_prev):
    B, L, _ = h.shape
    a = rms_norm(o_gla, p['gla_norm_g']).reshape(B, L, GLA_HEADS * GLA_DV) * jax.nn.silu(gr)
    d = (rms_norm(o_diff, p['diff_norm_g']) * (1.0 - lam_init)).reshape(B, L, DIFF_HEADS * 2 * HEAD_DIM)
    mixed = (jax.nn.sigmoid(ma) * a + jax.nn.sigmoid(mb) * d) @ p['w_o']
    h = layer_norm(DEEPNORM_ALPHA * h + mixed, p['ln1_g'], p['ln1_b'])
    f, conv_state = conv_ffn(h, p, conv_prev)
    h = layer_norm(DEEPNORM_ALPHA * h + f, p['ln2_g'], p['ln2_b'])
    return h, conv_state


def setup_inputs(seed: int = 0) -> dict:
    key = jax.random.key(seed)
    ks = jax.random.split(key, 32)
    f32 = jnp.float32
    n_pages = PAST_LEN // PAGE_SIZE
    n_used = DEC_BATCH * n_pages
    n_phys = n_used + n_used // 4
    two_f = 2 * D_FF

    def nrm(k, shape, scale):
        return jax.random.normal(k, shape, f32) * scale

    def gain(k, shape):
        return 1.0 + nrm(k, shape, 0.02)

    page_table = jax.random.permutation(ks[6], n_phys)[:n_used].reshape(DEC_BATCH, n_pages).astype(jnp.int32)
    return {
        'x_prompt': nrm(ks[0], (BATCH, SEQ, D_MODEL), 1.0),
        'x_sample': nrm(ks[1], (DEC_BATCH, DEC_SEQ, D_MODEL), 1.0),
        'cache_k': nrm(ks[2], (DEPTH, n_phys, PAGE_SIZE, 2 * DIFF_HEADS, HEAD_DIM), 1.0),
        'cache_v': nrm(ks[3], (DEPTH, n_phys, PAGE_SIZE, DIFF_HEADS, 2 * HEAD_DIM), 1.0),
        'state_gla': nrm(ks[4], (DEPTH, DEC_BATCH, GLA_HEADS, GLA_DK, GLA_DV), 1.0),
        'state_ffn_conv': nrm(ks[5], (DEPTH, DEC_BATCH, CONV_WIDTH - 1, two_f), 1.0),
        'page_table': page_table,
        'meta_tokens': nrm(ks[7], (N_META, D_MODEL), 1.0),
        'ln_in_g': gain(ks[8], (D_MODEL,)),
        'ln_in_b': nrm(ks[9], (D_MODEL,), 0.02),
        'w_in': nrm(ks[10], (DEPTH, D_MODEL, D_IN), D_MODEL ** -0.5),
        'w_gate_up': nrm(ks[11], (DEPTH, GLA_GATE_RANK, GLA_HEADS * GLA_DK), GLA_GATE_RANK ** -0.5),
        'b_gate': nrm(ks[12], (DEPTH, GLA_HEADS * GLA_DK), 0.1),
        'gla_norm_g': gain(ks[13], (DEPTH, GLA_DV)),
        'lambda_q1': nrm(ks[14], (DEPTH, HEAD_DIM), 0.1),
        'lambda_k1': nrm(ks[15], (DEPTH, HEAD_DIM), 0.1),
        'lambda_q2': nrm(ks[16], (DEPTH, HEAD_DIM), 0.1),
        'lambda_k2': nrm(ks[17], (DEPTH, HEAD_DIM), 0.1),
        'diff_norm_g': gain(ks[18], (DEPTH, 2 * HEAD_DIM)),
        'w_o': nrm(ks[19], (DEPTH, D_MODEL, D_MODEL), DEEPNORM_BETA * D_MODEL ** -0.5),
        'ln1_g': gain(ks[20], (DEPTH, D_MODEL)),
        'ln1_b': nrm(ks[21], (DEPTH, D_MODEL), 0.02),
        'w_up': nrm(ks[22], (DEPTH, D_MODEL, two_f), D_MODEL ** -0.5),
        'b_up': nrm(ks[23], (DEPTH, two_f), 0.02),
        'conv_w': nrm(ks[24], (DEPTH, CONV_WIDTH, two_f), CONV_WIDTH ** -0.5),
        'conv_b': nrm(ks[25], (DEPTH, two_f), 0.02),
        'w_down': nrm(ks[26], (DEPTH, D_FF, D_MODEL), DEEPNORM_BETA * D_FF ** -0.5),
        'ln2_g': gain(ks[27], (DEPTH, D_MODEL)),
        'ln2_b': nrm(ks[28], (DEPTH, D_MODEL), 0.02),
    }


def reference(x_prompt, x_sample, cache_k, cache_v, state_gla, state_ffn_conv, page_table,
              meta_tokens, ln_in_g, ln_in_b, w_in, w_gate_up, b_gate, gla_norm_g,
              lambda_q1, lambda_k1, lambda_q2, lambda_k2, diff_norm_g, w_o, ln1_g, ln1_b,
              w_up, b_up, conv_w, conv_b, w_down, ln2_g, ln2_b):
    B, S, _ = x_prompt.shape
    Bd, T, _ = x_sample.shape
    past = page_table.shape[1] * cache_k.shape[2]
    meta = jnp.broadcast_to(meta_tokens.astype(x_prompt.dtype)[None], (B, N_META, D_MODEL))
    hp = layer_norm(jnp.concatenate([meta, x_prompt], axis=1), ln_in_g, ln_in_b)
    hs = layer_norm(x_sample, ln_in_g, ln_in_b)
    pos_p = jnp.arange(N_META + S)
    pos_s = past + jnp.arange(T)
    k_p_l, v_p_l, g_p_l, c_p_l, k_s_l, v_s_l, g_s_l, c_s_l = [], [], [], [], [], [], [], []
    for l in range(DEPTH):
        p = {'gla_norm_g': gla_norm_g[l], 'diff_norm_g': diff_norm_g[l], 'w_o': w_o[l],
             'ln1_g': ln1_g[l], 'ln1_b': ln1_b[l], 'w_up': w_up[l], 'b_up': b_up[l],
             'conv_w': conv_w[l], 'conv_b': conv_b[l], 'w_down': w_down[l],
             'ln2_g': ln2_g[l], 'ln2_b': ln2_b[l]}
        lam_init = 0.8 - 0.6 * math.exp(-0.3 * l)
        lam = diff_lambda(lambda_q1[l], lambda_k1[l], lambda_q2[l], lambda_k2[l], lam_init)

        gq, gk, gv, ga, gr, dq, dk, dv, ma, mb = project_inputs(hp, pos_p, w_in[l], w_gate_up[l], b_gate[l])
        s_zero = jnp.zeros((B, GLA_HEADS, GLA_DK, GLA_DV), hp.dtype)
        o_meta, s_meta = gla_chunked(gq[:, :N_META], gk[:, :N_META], gv[:, :N_META], ga[:, :N_META], s_zero, N_META)
        o_real, s_p = gla_chunked(gq[:, N_META:], gk[:, N_META:], gv[:, N_META:], ga[:, N_META:], s_meta, GLA_CHUNK)
        o_gla = jnp.concatenate([o_meta, o_real], axis=1)
        o_diff = diff_attn_prompt(dq, dk, dv, lam)
        hp, conv_p = merge_and_ffn(hp, o_gla, gr, o_diff, ma, mb, lam_init, p,
                                   jnp.zeros((B, CONV_WIDTH - 1, 2 * D_FF), hp.dtype))
        k_p_l.append(dk)
        v_p_l.append(dv)
        g_p_l.append(s_p)
        c_p_l.append(conv_p)

        sq, sk, sv, sa, sr, eq, ek, ev, na, nb = project_inputs(hs, pos_s, w_in[l], w_gate_up[l], b_gate[l])
        o_gla_s, s_s = gla_chunked(sq, sk, sv, sa, state_gla[l], T)
        o_diff_s = diff_attn_sample(eq, ek, ev, cache_k, cache_v, page_table, l, lam)
        hs, conv_s = merge_and_ffn(hs, o_gla_s, sr, o_diff_s, na, nb, lam_init, p, state_ffn_conv[l])
        k_s_l.append(ek)
        v_s_l.append(ev)
        g_s_l.append(s_s)
        c_s_l.append(conv_s)

    y_prompt = hp[:, N_META:]
    y_sample = hs
    k_prompt = jnp.stack(k_p_l, 0)
    v_prompt = jnp.stack(v_p_l, 0)
    gla_prompt = jnp.stack(g_p_l, 0)
    conv_prompt = jnp.stack(c_p_l, 0)
    k_sample = jnp.stack(k_s_l, 0)
    v_sample = jnp.stack(v_s_l, 0)
    gla_sample = jnp.stack(g_s_l, 0)
    conv_sample = jnp.stack(c_s_l, 0)
    return (y_prompt, y_sample, k_prompt, v_prompt, gla_prompt, conv_prompt, k_sample, v_sample, gla_sample, conv_sample)
```

```python
import functools
import math

import jax
import jax.numpy as jnp
from jax import lax
from jax.experimental import pallas as pl
from jax.experimental.pallas import tpu as pltpu

F32 = jnp.float32
BF16 = jnp.bfloat16

N_META = 16
GLA_HEADS = 4
GLA_DK = 128
GLA_DV = 256
GLA_GATE_RANK = 16
GLA_GATE_TAU = 16.0
HEAD_DIM = 64
DIFF_HEADS = 8
ROPE_THETA = 10000.0
NORM_EPS = 1e-5
CONV_WIDTH = 3
DEPTH = 1
DEEPNORM_ALPHA = (2.0 * DEPTH) ** 0.25
LAM_INIT = 0.8 - 0.6 * math.exp(-0.3 * 0)

LANES = 128
SUBLANES = 8
VMEM_LIMIT = 56 * 1024 * 1024
GLA_CHUNK = 128
GLA_DIAG = 8
ATT_BLOCK = 128
PAGES_PER_STEP = 8
FF_CHUNK = 256
NEG_BIG = -1e30


def _const_spec(shape):
    return pl.BlockSpec(shape, lambda *_: (0,) * len(shape), pipeline_mode=pl.Buffered(1))


def _layer_norm(x, g, b):
    mu = jnp.mean(x, axis=-1, keepdims=True)
    xc = x - mu
    var = jnp.mean(xc * xc, axis=-1, keepdims=True)
    return xc * lax.rsqrt(var + NORM_EPS) * g + b


def _sigmoid(x):
    return 1.0 / (1.0 + jnp.exp(-x))


def _dot(a, b):
    return jnp.dot(a, b, preferred_element_type=F32)


def _dot_nt(a, b):
    return lax.dot_general(a, b, (((1,), (1,)), ((), ())), preferred_element_type=F32)


def _inproj_kernel(x_ref, g_ref, b_ref, cos_ref, sin_ref, w_ref, wga_ref, wgu_ref, bg_ref,
                   gq_o, gk_o, la_o, gr_o, ma_o, mb_o, kf_o, vf_o, gv_o, q_o, k_o, v_o):
    h = _layer_norm(x_ref[...], g_ref[...], b_ref[...])
    hb = h.astype(BF16)

    def seg(c0, n):
        return _dot(hb, w_ref[:, c0:c0 + n])

    gq_o[...] = seg(0, 512) * (GLA_DK ** -0.5)
    gk_o[...] = seg(512, 512)
    gv_o[...] = seg(1024, 1024).astype(BF16)
    gr_o[...] = seg(2048, 1024)

    ga = _dot(hb, wga_ref[...])
    xg = _dot(ga.astype(BF16), wgu_ref[...]) + bg_ref[...]
    la_o[...] = (jnp.minimum(xg, 0.0) - jnp.log1p(jnp.exp(-jnp.abs(xg)))) * (1.0 / GLA_GATE_TAU)

    cos = cos_ref[...]
    sin = sin_ref[...]
    lane = lax.broadcasted_iota(jnp.int32, cos.shape, 1)
    first_half = (lane % HEAD_DIM) < (HEAD_DIM // 2)

    def rope(z):
        outs = []
        for g in range(z.shape[1] // LANES):
            zg = z[:, g * LANES:(g + 1) * LANES]
            rot = jnp.where(first_half, pltpu.roll(zg, LANES - HEAD_DIM // 2, 1), pltpu.roll(zg, HEAD_DIM // 2, 1))
            outs.append(zg * cos + rot * sin)
        return jnp.concatenate(outs, axis=1)

    q_o[...] = (rope(seg(3072, 1024)) * (HEAD_DIM ** -0.5)).astype(BF16)
    kr = rope(seg(4096, 1024))
    kf_o[...] = kr
    k_o[...] = kr.astype(BF16)
    vv = seg(5120, 1024)
    vf_o[...] = vv
    v_o[...] = vv.astype(BF16)
    ma_o[...] = seg(6144, 1024)
    mb_o[...] = seg(7168, 1024)


def _inproj(x, ln_g, ln_b, cos_t, sin_t, w_main, w_ga, w_gu, b_gate, tm):
    R, D = x.shape
    period = cos_t.shape[0] // tm
    row = lambda w: pl.BlockSpec((tm, w), lambda i: (i, 0))
    tab = pl.BlockSpec((tm, LANES), lambda i: (i % period, 0))
    f32_out = lambda w: jax.ShapeDtypeStruct((R, w), F32)
    bf_out = lambda w: jax.ShapeDtypeStruct((R, w), BF16)
    return pl.pallas_call(
        _inproj_kernel,
        grid=(R // tm,),
        in_specs=[row(D), _const_spec((1, D)), _const_spec((1, D)), tab, tab,
                  _const_spec(w_main.shape), _const_spec(w_ga.shape), _const_spec(w_gu.shape),
                  _const_spec((1, 512))],
        out_specs=[row(512), row(512), row(512), row(1024), row(1024), row(1024), row(1024), row(1024),
                   row(1024), row(1024), row(1024), row(1024)],
        out_shape=[f32_out(512), f32_out(512), f32_out(512), f32_out(1024), f32_out(1024), f32_out(1024),
                   f32_out(1024), f32_out(1024), bf_out(1024), bf_out(1024), bf_out(1024), bf_out(1024)],
        compiler_params=pltpu.CompilerParams(dimension_semantics=("parallel",), vmem_limit_bytes=VMEM_LIMIT),
        name="inproj",
    )(x, ln_g.reshape(1, D), ln_b.reshape(1, D), cos_t, sin_t, w_main, w_ga, w_gu, b_gate.reshape(1, 512))


def _split3(x):
    hi = x.astype(BF16)
    r1 = x - hi.astype(F32)
    mid = r1.astype(BF16)
    lo = (r1 - mid.astype(F32)).astype(BF16)
    return hi, mid, lo


def _gla_intra(q, k, bc, C):
    row = lax.broadcasted_iota(jnp.int32, (C, C), 0)
    col = lax.broadcasted_iota(jnp.int32, (C, C), 1)
    rowk = lax.broadcasted_iota(jnp.int32, (C, GLA_DK), 0)
    att = jnp.zeros((C, C), F32)
    hs = C // 2
    while hs >= GLA_DIAG:
        grp = 2 * hs
        refs = jnp.concatenate(
            [jnp.broadcast_to(bc[g * grp + hs - 1:g * grp + hs, :], (grp, GLA_DK)) for g in range(C // grp)], axis=0)
        upper = (rowk % grp) >= hs
        qs = (q * jnp.exp(jnp.where(upper, bc - refs, NEG_BIG))).astype(BF16)
        ks = (k * jnp.exp(jnp.where(upper, NEG_BIG, refs - bc))).astype(BF16)
        blk = _dot_nt(qs, ks)
        keep = ((row // grp) == (col // grp)) & ((row % grp) >= hs) & ((col % grp) < hs)
        att = att + jnp.where(keep, blk, 0.0)
        hs //= 2
    d = min(C, GLA_DIAG)
    nb = C // d
    k3 = k.reshape(nb, d, GLA_DK)
    b3 = bc.reshape(nb, d, GLA_DK)
    for jj in range(d):
        kb = jnp.broadcast_to(k3[:, jj:jj + 1, :], (nb, d, GLA_DK)).reshape(C, GLA_DK)
        bb = jnp.broadcast_to(b3[:, jj:jj + 1, :], (nb, d, GLA_DK)).reshape(C, GLA_DK)
        e = jnp.where((rowk % d) >= jj, bc - bb, NEG_BIG)
        colsum = jnp.sum(q * kb * jnp.exp(e), axis=1, keepdims=True)
        att = att + jnp.where(col == (row // d) * d + jj, colsum, 0.0)
    return att


def _gla_chunk(q, k, la, v, S, C):
    row = lax.broadcasted_iota(jnp.int32, (C, C), 0)
    col = lax.broadcasted_iota(jnp.int32, (C, C), 1)
    tri = jnp.where(row >= col, 1.0, 0.0).astype(BF16)
    hi, mid, lo = _split3(la)
    bc = _dot(tri, hi) + _dot(tri, mid) + _dot(tri, lo)
    o = _dot((q * jnp.exp(bc)).astype(BF16), S.astype(BF16))
    att = _gla_intra(q, k, bc, C)
    o = o + _dot(att.astype(BF16), v)
    bct = bc.T
    blt = bct[:, C - 1:C]
    kdt = (k.T * jnp.exp(blt - bct)).astype(BF16)
    S_new = jnp.exp(blt) * S + _dot(kdt, v)
    return o, S_new


def _gla_kernel(q_ref, k_ref, la_ref, v_ref, s0_ref, o_ref, s_ref, s_scr, *, n_full, tail):
    C = GLA_CHUNK
    s_scr[...] = s0_ref[0, 0]

    def do_chunk(r0, skip):
        sl = pl.ds(r0, C)
        q, k, la, v = q_ref[0, sl, :], k_ref[0, sl, :], la_ref[0, sl, :], v_ref[0, sl, :]
        if skip:
            live = lax.broadcasted_iota(jnp.int32, (C, GLA_DK), 0) >= skip
            live_v = lax.broadcasted_iota(jnp.int32, (C, GLA_DV), 0) >= skip
            q, k, la = jnp.where(live, q, 0.0), jnp.where(live, k, 0.0), jnp.where(live, la, 0.0)
            v = jnp.where(live_v, v, jnp.zeros_like(v))
        o, S_new = _gla_chunk(q, k, la, v, s_scr[...], C)
        if skip:
            o_ref[0, pl.ds(r0 + skip, C - skip), :] = o[skip:]
        else:
            o_ref[0, sl, :] = o
        s_scr[...] = S_new

    def body(i, carry):
        do_chunk(pl.multiple_of(i * C, C), 0)
        return carry

    lax.fori_loop(0, n_full, body, 0)
    if tail:
        do_chunk((n_full - 1) * C + tail, C - tail)
    s_ref[0, 0] = s_scr[...]


def _gla(q, k, la, v, s0):
    B, L, _ = q.shape
    n_full, tail = divmod(L, GLA_CHUNK)
    qk = pl.BlockSpec((1, L, GLA_DK), lambda b, h: (b, 0, h))
    vv = pl.BlockSpec((1, L, GLA_DV), lambda b, h: (b, 0, h))
    st = pl.BlockSpec((1, 1, GLA_DK, GLA_DV), lambda b, h: (b, h, 0, 0))
    return pl.pallas_call(
        functools.partial(_gla_kernel, n_full=n_full, tail=tail),
        grid=(B, GLA_HEADS),
        in_specs=[qk, qk, qk, vv, st],
        out_specs=[vv, st],
        out_shape=[jax.ShapeDtypeStruct((B, L, GLA_HEADS * GLA_DV), F32),
                   jax.ShapeDtypeStruct((B, GLA_HEADS, GLA_DK, GLA_DV), F32)],
        scratch_shapes=[pltpu.VMEM((GLA_DK, GLA_DV), F32)],
        compiler_params=pltpu.CompilerParams(dimension_semantics=("parallel", "parallel"),
                                             vmem_limit_bytes=VMEM_LIMIT),
        name="gla",
    )(q, k, la, v, s0)


def _stack_maps(q):
    lane = lax.broadcasted_iota(jnp.int32, q.shape, 1)
    zero = jnp.zeros_like(q)
    return jnp.concatenate([jnp.where(lane < HEAD_DIM, q, zero), jnp.where(lane >= HEAD_DIM, q, zero)], axis=0)


def _softmax_step(s, v, m, l, acc):
    m_new = jnp.maximum(m, jnp.max(s, axis=1, keepdims=True))
    p = jnp.exp(s - m_new)
    corr = jnp.exp(m - m_new)
    l = l * corr + jnp.sum(p, axis=1, keepdims=True)
    acc = acc * corr + _dot(p.astype(BF16), v)
    return m_new, l, acc


def _diff_prompt_kernel(lam_ref, q_ref, k_ref, v_ref, o_ref, *, n_full, tail):
    lam = lam_ref[0]

    C = ATT_BLOCK

    def q_block(r0, n_kv, skip):
        q2 = _stack_maps(q_ref[0, pl.ds(r0, C), :])
        init = (jnp.full((2 * C, 1), -jnp.inf, F32), jnp.zeros((2 * C, 1), F32), jnp.zeros((2 * C, LANES), F32))

        def kv_step(j, carry):
            sl = pl.ds(pl.multiple_of(j * C, C), C)
            return _softmax_step(_dot_nt(q2, k_ref[0, sl, :]), v_ref[0, sl, :], *carry)

        carry = lax.fori_loop(0, n_kv, kv_step, init)
        sl = pl.ds(r0 + skip, C - skip)
        s = _dot_nt(q2, k_ref[0, sl, :])
        row = lax.broadcasted_iota(jnp.int32, s.shape, 0) % C
        col = lax.broadcasted_iota(jnp.int32, s.shape, 1)
        s = jnp.where(col + skip <= row, s, -jnp.inf)
        m, l, acc = _softmax_step(s, v_ref[0, sl, :], *carry)
        o = acc / l
        o = o[:C] - lam * o[C:]
        o_ref[0, sl, :] = o[skip:]

    def body(i, carry):
        q_block(pl.multiple_of(i * C, C), i, 0)
        return carry

    lax.fori_loop(0, n_full, body, 0)
    if tail:
        q_block((n_full - 1) * C + tail, n_full, C - tail)


def _diff_prompt(lam, q, k, v):
    B, L, W = q.shape
    n_full, tail = divmod(L, ATT_BLOCK)
    blk = pl.BlockSpec((1, L, LANES), lambda b, h, lam_ref: (b, 0, h))
    return pl.pallas_call(
        functools.partial(_diff_prompt_kernel, n_full=n_full, tail=tail),
        grid_spec=pltpu.PrefetchScalarGridSpec(
            num_scalar_prefetch=1, grid=(B, DIFF_HEADS), in_specs=[blk, blk, blk], out_specs=blk),
        out_shape=jax.ShapeDtypeStruct((B, L, W), F32),
        compiler_params=pltpu.CompilerParams(dimension_semantics=("parallel", "parallel"),
                                             vmem_limit_bytes=VMEM_LIMIT),
        name="diff_prompt",
    )(lam, q, k, v)


def _diff_sample_kernel(pt_ref, lam_ref, q_ref, kn_ref, vn_ref, *refs, T):
    G = PAGES_PER_STEP
    k_refs, v_refs = refs[:G], refs[G:2 * G]
    o_ref, m_scr, l_scr, acc_scr = refs[2 * G:]
    p_id = pl.program_id(1)
    q = q_ref[0]
    R = q.shape[0]

    @pl.when(p_id == 0)
    def _():
        s = _dot_nt(q, kn_ref[0])
        t_row = lax.broadcasted_iota(jnp.int32, s.shape, 0) % T
        t_col = lax.broadcasted_iota(jnp.int32, s.shape, 1)
        s = jnp.where(t_col <= t_row, s, -jnp.inf)
        m = jnp.max(s, axis=1, keepdims=True)
        p = jnp.exp(s - m)
        m_scr[...] = m
        l_scr[...] = jnp.sum(p, axis=1, keepdims=True)
        acc_scr[...] = _dot(p.astype(BF16), vn_ref[0])

    s = jnp.concatenate([_dot_nt(q, k_refs[g][0].astype(BF16)) for g in range(G)], axis=1)
    m_old = m_scr[...]
    m_new = jnp.maximum(m_old, jnp.max(s, axis=1, keepdims=True))
    p = jnp.exp(s - m_new)
    corr = jnp.exp(m_old - m_new)
    pb = p.astype(BF16)
    ps = v_refs[0].shape[1]
    pv = _dot(pb[:, :ps], v_refs[0][0].astype(BF16))
    for g in range(1, G):
        pv = pv + _dot(pb[:, g * ps:(g + 1) * ps], v_refs[g][0].astype(BF16))
    m_scr[...] = m_new
    l_scr[...] = l_scr[...] * corr + jnp.sum(p, axis=1, keepdims=True)
    acc_scr[...] = acc_scr[...] * corr + pv

    @pl.when(p_id == pl.num_programs(1) - 1)
    def _():
        lam = lam_ref[0]
        o = acc_scr[...] / l_scr[...]
        pieces = []
        for h in range(DIFF_HEADS):
            tile = o[h * 2 * T:(h + 1) * 2 * T, h * LANES:(h + 1) * LANES]
            pieces.append(tile[:T] - lam * tile[T:])
        o_ref[0] = jnp.concatenate(pieces, axis=1)


def _diff_sample(page_table, lam, q_rows, k_new, v_new, cache_k, cache_v, T):
    Bd, R, W = q_rows.shape
    n_pages = page_table.shape[1]
    G = PAGES_PER_STEP
    ps = cache_k.shape[1]
    Tp = k_new.shape[1]
    per_b = lambda shape: pl.BlockSpec((1,) + shape, lambda b, p, pt, lam_ref: (b, 0, 0))
    page = lambda g: pl.BlockSpec((1, ps, W), lambda b, p, pt, lam_ref, g=g: (pt[b, p * G + g], 0, 0))
    return pl.pallas_call(
        functools.partial(_diff_sample_kernel, T=T),
        grid_spec=pltpu.PrefetchScalarGridSpec(
            num_scalar_prefetch=2, grid=(Bd, n_pages // G),
            in_specs=[per_b((R, W)), per_b((Tp, W)), per_b((Tp, W))]
                     + [page(g) for g in range(G)] + [page(g) for g in range(G)],
            out_specs=per_b((T, W)),
            scratch_shapes=[pltpu.VMEM((R, 1), F32), pltpu.VMEM((R, 1), F32), pltpu.VMEM((R, W), F32)]),
        out_shape=jax.ShapeDtypeStruct((Bd, T, W), F32),
        compiler_params=pltpu.CompilerParams(dimension_semantics=("parallel", "arbitrary"),
                                             vmem_limit_bytes=VMEM_LIMIT),
        name="diff_sample",
    )(page_table, lam, q_rows, k_new, v_new, *([cache_k] * G), *([cache_v] * G))


def _head_rms(x, g, width):
    outs = []
    for h in range(x.shape[1] // width):
        xh = x[:, h * width:(h + 1) * width]
        outs.append(xh * lax.rsqrt(jnp.mean(xh * xh, axis=1, keepdims=True) + NORM_EPS) * g)
    return jnp.concatenate(outs, axis=1)


def _post_kernel(*refs, seq_len, has_state):
    (x_ref, og_ref, gr_ref, od_ref, ma_ref, mb_ref, lng_ref, lnb_ref, gg_ref, dg_ref, wo_ref,
     l1g_ref, l1b_ref, wug_ref, wuu_ref, bug_ref, buu_ref, cwg_ref, cwu_ref, cbg_ref, cbu_ref,
     wd_ref, l2g_ref, l2b_ref) = refs[:24]
    if has_state:
        p1g_ref, p1u_ref, p2g_ref, p2u_ref, y_ref, ug_ref, uu_ref, h1_scr, acc_scr, cg_scr, cu_scr = refs[24:]
    else:
        y_ref, ug_ref, uu_ref, h1_scr, acc_scr, cg_scr, cu_scr = refs[24:]
    tm = x_ref.shape[0]
    n_chunks = wug_ref.shape[0]

    h = _layer_norm(x_ref[...], lng_ref[...], lnb_ref[...])
    a = _head_rms(og_ref[...], gg_ref[...], GLA_DV)
    a = a * (gr_ref[...] * _sigmoid(gr_ref[...]))
    d = _head_rms(od_ref[...], dg_ref[...], 2 * HEAD_DIM) * (1.0 - LAM_INIT)
    mixed = _sigmoid(ma_ref[...]) * a + _sigmoid(mb_ref[...]) * d
    h1 = _layer_norm(DEEPNORM_ALPHA * h + _dot(mixed.astype(BF16), wo_ref[...]), l1g_ref[...], l1b_ref[...])
    h1_scr[...] = h1
    h1b = h1.astype(BF16)
    acc_scr[...] = jnp.zeros_like(acc_scr)

    @pl.when(pl.program_id(0) == 0)
    def _():
        cg_scr[...] = jnp.zeros_like(cg_scr)
        cu_scr[...] = jnp.zeros_like(cu_scr)

    pos =(pl.program_id(0) * tm + lax.broadcasted_iota(jnp.int32, (tm, FF_CHUNK), 0)) % seq_len
    row8 = lax.broadcasted_iota(jnp.int32, (SUBLANES, FF_CHUNK), 0)

    def shifted(u, s, prev8, state):
        r = pltpu.roll(u, s, 0)
        head = jnp.where(row8 < s, pltpu.roll(prev8, s, 0), r[:SUBLANES])
        r = jnp.concatenate([head, r[SUBLANES:]], axis=0)
        return jnp.where(pos >= s, r, state)

    def conv(u, cw, cb, prev8, st1, st2):
        return cb + cw[0:1] * shifted(u, 2, prev8, st2) + cw[1:2] * shifted(u, 1, prev8, st1) + cw[2:3] * u

    def chunk(c, carry):
        u_g = _dot(h1b, wug_ref[c]) + bug_ref[c]
        u_u = _dot(h1b, wuu_ref[c]) + buu_ref[c]
        if has_state:
            st = (p1g_ref[c], p2g_ref[c], p1u_ref[c], p2u_ref[c])
        else:
            st = (0.0, 0.0, 0.0, 0.0)
        gate = conv(u_g, cwg_ref[c], cbg_ref[c], cg_scr[c], st[0], st[1])
        up = conv(u_u, cwu_ref[c], cbu_ref[c], cu_scr[c], st[2], st[3])
        cg_scr[c] = u_g[tm - SUBLANES:]
        cu_scr[c] = u_u[tm - SUBLANES:]
        if has_state:
            ug_ref[c] = u_g
            uu_ref[c] = u_u
        else:
            ug_ref[0, c] = u_g[tm - SUBLANES:]
            uu_ref[0, c] = u_u[tm - SUBLANES:]
        gelu = 0.5 * gate * (1.0 + jnp.tanh(math.sqrt(2.0 / math.pi) * (gate + 0.044715 * (gate * gate * gate))))
        acc_scr[...] += _dot((gelu * up).astype(BF16), wd_ref[c])
        return carry

    lax.fori_loop(0, n_chunks, chunk, 0)
    y_ref[...] = _layer_norm(DEEPNORM_ALPHA * h1_scr[...] + acc_scr[...], l2g_ref[...], l2b_ref[...])


def _post(x, og, gr, od, ma, mb, vecs, wo, wug, wuu, bug, buu, cwg, cwu, cbg, cbu, wd, tm, seq_len, state=None):
    R, D = x.shape
    n_tiles = R // tm
    nc = wug.shape[0]
    has_state = state is not None
    row = pl.BlockSpec((tm, D), lambda i: (i, 0))
    in_specs = [row] * 6 + [_const_spec((1, D)), _const_spec((1, D)), _const_spec((1, GLA_DV)),
                            _const_spec((1, 2 * HEAD_DIM)), _const_spec(wo.shape), _const_spec((1, D)),
                            _const_spec((1, D)), _const_spec(wug.shape), _const_spec(wuu.shape),
                            _const_spec(bug.shape), _const_spec(buu.shape), _const_spec(cwg.shape),
                            _const_spec(cwu.shape), _const_spec(cbg.shape), _const_spec(cbu.shape),
                            _const_spec(wd.shape), _const_spec((1, D)), _const_spec((1, D))]
    args = [x, og, gr, od, ma, mb] + list(vecs[:4]) + [wo] + list(vecs[4:6]) + [wug, wuu, bug, buu, cwg, cwu, cbg,
                                                                                  cbu, wd] + list(vecs[6:8])
    if has_state:
        in_specs += [pl.BlockSpec((nc, tm, FF_CHUNK), lambda i: (0, i, 0))] * 4
        args += list(state)
        u_spec = pl.BlockSpec((nc, tm, FF_CHUNK), lambda i: (0, i, 0))
        u_shape = jax.ShapeDtypeStruct((nc, R, FF_CHUNK), F32)
    else:
        u_spec = pl.BlockSpec((1, nc, SUBLANES, FF_CHUNK), lambda i: (i, 0, 0, 0))
        u_shape = jax.ShapeDtypeStruct((n_tiles, nc, SUBLANES, FF_CHUNK), F32)
    return pl.pallas_call(
        functools.partial(_post_kernel, seq_len=seq_len, has_state=has_state),
        grid=(n_tiles,),
        in_specs=in_specs,
        out_specs=[row, u_spec, u_spec],
        out_shape=[jax.ShapeDtypeStruct((R, D), F32), u_shape, u_shape],
        scratch_shapes=[pltpu.VMEM((tm, D), F32), pltpu.VMEM((tm, D), F32),
                        pltpu.VMEM((nc, SUBLANES, FF_CHUNK), F32), pltpu.VMEM((nc, SUBLANES, FF_CHUNK), F32)],
        compiler_params=pltpu.CompilerParams(dimension_semantics=("arbitrary",), vmem_limit_bytes=VMEM_LIMIT),
        name="post_state" if has_state else "post",
    )(*args)


def _rope_tables(pos):
    inv_freq = 1.0 / (ROPE_THETA ** (jnp.arange(0, HEAD_DIM, 2, dtype=F32) / HEAD_DIM))
    ang = pos.astype(F32)[:, None] * inv_freq[None, :]
    ang = jnp.concatenate([ang, ang], axis=-1)
    sign = jnp.concatenate([-jnp.ones((HEAD_DIM // 2,), F32), jnp.ones((HEAD_DIM // 2,), F32)])
    return jnp.tile(jnp.cos(ang), (1, 2)), jnp.tile(jnp.sin(ang) * sign, (1, 2))


def _chunk_cols(w, nc):
    return jnp.moveaxis(w.reshape(w.shape[:-1] + (nc, FF_CHUNK)), -2, 0)


def _row_tile(L, target):
    best = SUBLANES
    for t in range(SUBLANES, target + 1, SUBLANES):
        if L % t == 0:
            best = t
    return best


def kernel(x_prompt, x_sample, cache_k, cache_v, state_gla, state_ffn_conv, page_table, meta_tokens, ln_in_g, ln_in_b, w_in, w_gate_up, b_gate, gla_norm_g, lambda_q1, lambda_k1, lambda_q2, lambda_k2, diff_norm_g, w_o, ln1_g, ln1_b, w_up, b_up, conv_w, conv_b, w_down, ln2_g, ln2_b):
    B, S, D = x_prompt.shape
    Bd, T, _ = x_sample.shape
    L = N_META + S
    n_phys, page_size = cache_k.shape[1], cache_k.shape[2]
    past = page_table.shape[1] * page_size
    d_ff = w_down.shape[1]
    nc = d_ff // FF_CHUNK
    l = 0

    wi = w_in[l]
    offs = [0, 512, 1024, 2048, 3072, 3088, 4112, 5136, 6160, 7184, 8208]
    seg = lambda i: wi[:, offs[i]:offs[i + 1]]
    w_main = jnp.concatenate([seg(0), seg(1), seg(2), seg(3), seg(5), seg(6), seg(7), seg(8), seg(9)],
                             axis=1).astype(BF16)
    w_ga = jnp.pad(seg(4), ((0, 0), (0, LANES - GLA_GATE_RANK))).astype(BF16)
    w_gu = jnp.pad(w_gate_up[l], ((0, LANES - GLA_GATE_RANK), (0, 0))).astype(BF16)
    lam = (jnp.exp(jnp.sum(lambda_q1[l] * lambda_k1[l])) - jnp.exp(jnp.sum(lambda_q2[l] * lambda_k2[l]))
           + LAM_INIT).reshape(1).astype(F32)
    r1 = lambda v: v.reshape(1, -1)
    vecs = [r1(ln_in_g), r1(ln_in_b), r1(gla_norm_g[l]), r1(diff_norm_g[l]), r1(ln1_g[l]), r1(ln1_b[l]),
            r1(ln2_g[l]), r1(ln2_b[l])]
    wo = w_o[l].astype(BF16)
    wug = _chunk_cols(w_up[l][:, :d_ff], nc).astype(BF16)
    wuu = _chunk_cols(w_up[l][:, d_ff:], nc).astype(BF16)
    bug = _chunk_cols(b_up[l][None, :d_ff], nc)
    buu = _chunk_cols(b_up[l][None, d_ff:], nc)
    cwg = _chunk_cols(conv_w[l][:, :d_ff], nc)
    cwu = _chunk_cols(conv_w[l][:, d_ff:], nc)
    cbg = _chunk_cols(conv_b[l][None, :d_ff], nc)
    cbu = _chunk_cols(conv_b[l][None, d_ff:], nc)
    wd = w_down[l].reshape(nc, FF_CHUNK, D).astype(BF16)
    ffn_w = (wo, wug, wuu, bug, buu, cwg, cwu, cbg, cbu, wd)

    xp = jnp.concatenate([jnp.broadcast_to(meta_tokens[None], (B, N_META, D)), x_prompt], axis=1).reshape(B * L, D)
    tm = _row_tile(L, 344)
    cos_p, sin_p = _rope_tables(jnp.arange(L))
    gq, gk, la, gr, ma, mb, kf, vf, gv, qb, kb, vb = _inproj(xp, ln_in_g, ln_in_b, cos_p, sin_p, w_main, w_ga, w_gu,
                                                            b_gate[l], tm)
    b3 = lambda a: a.reshape(B, L, a.shape[-1])
    o_gla, s_p = _gla(b3(gq), b3(gk), b3(la), b3(gv), jnp.zeros((B, GLA_HEADS, GLA_DK, GLA_DV), F32))
    o_diff = _diff_prompt(lam, b3(qb), b3(kb), b3(vb))
    y_full, utg, utu = _post(xp, o_gla.reshape(B * L, D), gr, o_diff.reshape(B * L, D), ma, mb, vecs, *ffn_w,
                             tm=tm, seq_len=L)
    y_prompt = y_full.reshape(B, L, D)[:, N_META:]
    tiles_per_seq = L // tm
    last = lambda u: u.reshape(B, tiles_per_seq, nc, SUBLANES, FF_CHUNK)[:, -1, :, SUBLANES - (CONV_WIDTH - 1):, :]
    conv_p = jnp.concatenate([jnp.moveaxis(last(utg), 1, 2).reshape(B, CONV_WIDTH - 1, d_ff),
                              jnp.moveaxis(last(utu), 1, 2).reshape(B, CONV_WIDTH - 1, d_ff)], axis=-1)
    k_prompt = kf.reshape(1, B, L, 2 * DIFF_HEADS, HEAD_DIM)
    v_prompt = vf.reshape(1, B, L, DIFF_HEADS, 2 * HEAD_DIM)

    Rs = Bd * T
    xs = x_sample.reshape(Rs, D)
    cos_s, sin_s = _rope_tables(past + jnp.arange(T))
    cos_s, sin_s = jnp.tile(cos_s, (Bd, 1)), jnp.tile(sin_s, (Bd, 1))
    sq, sk, sa, sr, na, nb, ekf, evf, sv, eqb, ekb, evb = _inproj(xs, ln_in_g, ln_in_b, cos_s, sin_s, w_main, w_ga,
                                                                  w_gu, b_gate[l], Rs)
    Tp = GLA_CHUNK
    padT = lambda a: jnp.pad(a.reshape(Bd, T, a.shape[-1]), ((0, 0), (Tp - T, 0), (0, 0)))
    o_gla_s, s_s = _gla(padT(sq), padT(sk), padT(sa), padT(sv), state_gla[l])
    o_gla_s = o_gla_s[:, Tp - T:].reshape(Rs, D)
    q4 = eqb.reshape(Bd, T, 2 * DIFF_HEADS, HEAD_DIM)
    eye = jnp.eye(2 * DIFF_HEADS, dtype=BF16)
    q_rows = (q4.transpose(0, 2, 1, 3)[:, :, :, None, :] * eye[None, :, None, :, None]).reshape(
        Bd, 2 * DIFF_HEADS * T, D)
    Tk = 2 * SUBLANES
    padK = lambda a: jnp.pad(a.reshape(Bd, T, D), ((0, 0), (0, Tk - T), (0, 0)))
    o_diff_s = _diff_sample(page_table, lam, q_rows, padK(ekb), padK(evb),
                            cache_k[l].reshape(n_phys, page_size, D), cache_v[l].reshape(n_phys, page_size, D), T)
    st = state_ffn_conv[l]
    zero = jnp.zeros((Bd, 1, 2 * d_ff), F32)
    prev1 = jnp.concatenate([st[:, 1:2]] + [zero] * (T - 1), axis=1).reshape(Rs, 2 * d_ff)
    prev2 = jnp.concatenate([st[:, 0:1], st[:, 1:2]] + [zero] * (T - 2), axis=1).reshape(Rs, 2 * d_ff)
    state = (_chunk_cols(prev1[:, :d_ff], nc), _chunk_cols(prev1[:, d_ff:], nc),
             _chunk_cols(prev2[:, :d_ff], nc), _chunk_cols(prev2[:, d_ff:], nc))
    y_s, usg, usu = _post(xs, o_gla_s, sr, o_diff_s.reshape(Rs, D), na, nb, vecs, *ffn_w, tm=Rs, seq_len=T,
                          state=state)
    y_sample = y_s.reshape(Bd, T, D)
    unchunk = lambda u: jnp.moveaxis(u, 0, 1).reshape(Bd, T, d_ff)[:, T - (CONV_WIDTH - 1):]
    conv_s = jnp.concatenate([unchunk(usg), unchunk(usu)], axis=-1)
    k_sample = ekf.reshape(1, Bd, T, 2 * DIFF_HEADS, HEAD_DIM)
    v_sample = evf.reshape(1, Bd, T, DIFF_HEADS, 2 * HEAD_DIM)

    return (y_prompt, y_sample, k_prompt, v_prompt, s_p[None], conv_p[None], k_sample, v_sample, s_s[None],
            conv_s[None])
```

```python
import functools
import math

import jax
import jax.numpy as jnp
from jax import lax
from jax.experimental import pallas as pl
from jax.experimental.pallas import tpu as pltpu

F32 = jnp.float32
BF16 = jnp.bfloat16

N_META = 16
GLA_HEADS = 4
GLA_DK = 128
GLA_DV = 256
GLA_GATE_RANK = 16
GLA_GATE_TAU = 16.0
HEAD_DIM = 64
DIFF_HEADS = 8
ROPE_THETA = 10000.0
NORM_EPS = 1e-5
CONV_WIDTH = 3
DEPTH = 1
DEEPNORM_ALPHA = (2.0 * DEPTH) ** 0.25
LAM_INIT = 0.8 - 0.6 * math.exp(-0.3 * 0)

LANES = 128
SUBLANES = 8
VMEM_LIMIT = 56 * 1024 * 1024
GLA_CHUNK = 128
GLA_DIAG = 8
ATT_BLOCK = 128
ATT_GROUP = 4
PAGES_PER_STEP = 8
NEG_BIG = -1e30


def _const_spec(shape):
    return pl.BlockSpec(shape, lambda *_: (0,) * len(shape), pipeline_mode=pl.Buffered(1))


def _layer_norm(x, g, b):
    mu = jnp.mean(x, axis=-1, keepdims=True)
    xc = x - mu
    var = jnp.mean(xc * xc, axis=-1, keepdims=True)
    return xc * lax.rsqrt(var + NORM_EPS) * g + b


def _sigmoid(x):
    return 1.0 / (1.0 + jnp.exp(-x))


def _dot(a, b):
    return jnp.dot(a, b, preferred_element_type=F32)


def _dot_nt(a, b):
    return lax.dot_general(a, b, (((1,), (1,)), ((), ())), preferred_element_type=F32)


def _inproj_kernel(x_ref, g_ref, b_ref, cos_ref, sin_ref, w_ref, wga_ref, wgu_ref, bg_ref,
                   gq_o, gk_o, la_o, gr_o, ma_o, mb_o, kf_o, vf_o, gv_o, q_o, k_o, v_o):
    h = _layer_norm(x_ref[...], g_ref[...], b_ref[...])
    hb = h.astype(BF16)

    def seg(c0, n):
        return _dot(hb, w_ref[:, c0:c0 + n])

    gq_o[...] = seg(0, 512) * (GLA_DK ** -0.5)
    gk_o[...] = seg(512, 512)
    gv_o[...] = seg(1024, 1024).astype(BF16)
    gr_o[...] = seg(2048, 1024)

    ga = _dot(hb, wga_ref[...])
    xg = _dot(ga.astype(BF16), wgu_ref[...]) + bg_ref[...]
    la_o[...] = (jnp.minimum(xg, 0.0) - jnp.log1p(jnp.exp(-jnp.abs(xg)))) * (1.0 / GLA_GATE_TAU)

    cos = cos_ref[...]
    sin = sin_ref[...]
    lane = lax.broadcasted_iota(jnp.int32, cos.shape, 1)
    first_half = (lane % HEAD_DIM) < (HEAD_DIM // 2)

    def rope(z):
        outs = []
        for g in range(z.shape[1] // LANES):
            zg = z[:, g * LANES:(g + 1) * LANES]
            rot = jnp.where(first_half, pltpu.roll(zg, LANES - HEAD_DIM // 2, 1), pltpu.roll(zg, HEAD_DIM // 2, 1))
            outs.append(zg * cos + rot * sin)
        return jnp.concatenate(outs, axis=1)

    q_o[...] = (rope(seg(3072, 1024)) * (HEAD_DIM ** -0.5)).astype(BF16)
    kr = rope(seg(4096, 1024))
    kf_o[...] = kr
    k_o[...] = kr.astype(BF16)
    vv = seg(5120, 1024)
    vf_o[...] = vv
    v_o[...] = vv.astype(BF16)
    ma_o[...] = seg(6144, 1024)
    mb_o[...] = seg(7168, 1024)


def _inproj(x, ln_g, ln_b, cos_t, sin_t, w_main, w_ga, w_gu, b_gate, tm):
    R, D = x.shape
    period = cos_t.shape[0] // tm
    row = lambda w: pl.BlockSpec((tm, w), lambda i: (i, 0))
    tab = pl.BlockSpec((tm, LANES), lambda i: (i % period, 0))
    f32_out = lambda w: jax.ShapeDtypeStruct((R, w), F32)
    bf_out = lambda w: jax.ShapeDtypeStruct((R, w), BF16)
    return pl.pallas_call(
        _inproj_kernel,
        grid=(R // tm,),
        in_specs=[row(D), _const_spec((1, D)), _const_spec((1, D)), tab, tab,
                  _const_spec(w_main.shape), _const_spec(w_ga.shape), _const_spec(w_gu.shape),
                  _const_spec((1, 512))],
        out_specs=[row(512), row(512), row(512), row(1024), row(1024), row(1024), row(1024), row(1024),
                   row(1024), row(1024), row(1024), row(1024)],
        out_shape=[f32_out(512), f32_out(512), f32_out(512), f32_out(1024), f32_out(1024), f32_out(1024),
                   f32_out(1024), f32_out(1024), bf_out(1024), bf_out(1024), bf_out(1024), bf_out(1024)],
        compiler_params=pltpu.CompilerParams(dimension_semantics=("parallel",), vmem_limit_bytes=VMEM_LIMIT),
        name="inproj",
    )(x, ln_g.reshape(1, D), ln_b.reshape(1, D), cos_t, sin_t, w_main, w_ga, w_gu, b_gate.reshape(1, 512))


def _split3(x):
    hi = x.astype(BF16)
    r1 = x - hi.astype(F32)
    mid = r1.astype(BF16)
    lo = (r1 - mid.astype(F32)).astype(BF16)
    return hi, mid, lo


def _gla_intra(q, k, bc, C):
    row = lax.broadcasted_iota(jnp.int32, (C, C), 0)
    col = lax.broadcasted_iota(jnp.int32, (C, C), 1)
    rowk = lax.broadcasted_iota(jnp.int32, (C, GLA_DK), 0)
    att = jnp.zeros((C, C), F32)
    hs = C // 2
    while hs >= GLA_DIAG:
        grp = 2 * hs
        refs = jnp.concatenate(
            [jnp.broadcast_to(bc[g * grp + hs - 1:g * grp + hs, :], (grp, GLA_DK)) for g in range(C // grp)], axis=0)
        upper = (rowk % grp) >= hs
        qs = (q * jnp.exp(jnp.where(upper, bc - refs, NEG_BIG))).astype(BF16)
        ks = (k * jnp.exp(jnp.where(upper, NEG_BIG, refs - bc))).astype(BF16)
        blk = _dot_nt(qs, ks)
        keep = ((row // grp) == (col // grp)) & ((row % grp) >= hs) & ((col % grp) < hs)
        att = att + jnp.where(keep, blk, 0.0)
        hs //= 2
    d = min(C, GLA_DIAG)
    nb = C // d
    k3 = k.reshape(nb, d, GLA_DK)
    b3 = bc.reshape(nb, d, GLA_DK)
    for jj in range(d):
        kb = jnp.broadcast_to(k3[:, jj:jj + 1, :], (nb, d, GLA_DK)).reshape(C, GLA_DK)
        bb = jnp.broadcast_to(b3[:, jj:jj + 1, :], (nb, d, GLA_DK)).reshape(C, GLA_DK)
        e = jnp.where((rowk % d) >= jj, bc - bb, NEG_BIG)
        colsum = jnp.sum(q * kb * jnp.exp(e), axis=1, keepdims=True)
        att = att + jnp.where(col == (row // d) * d + jj, colsum, 0.0)
    return att


def _gla_chunk(q, k, la, v, S, C):
    row = lax.broadcasted_iota(jnp.int32, (C, C), 0)
    col = lax.broadcasted_iota(jnp.int32, (C, C), 1)
    tri = jnp.where(row >= col, 1.0, 0.0).astype(BF16)
    hi, mid, lo = _split3(la)
    bc = _dot(tri, hi) + _dot(tri, mid) + _dot(tri, lo)
    o = _dot((q * jnp.exp(bc)).astype(BF16), S.astype(BF16))
    att = _gla_intra(q, k, bc, C)
    o = o + _dot(att.astype(BF16), v)
    bct = bc.T
    blt = bct[:, C - 1:C]
    kdt = (k.T * jnp.exp(blt - bct)).astype(BF16)
    S_new = jnp.exp(blt) * S + _dot(kdt, v)
    return o, S_new


def _gla_kernel(q_ref, k_ref, la_ref, v_ref, s0_ref, o_ref, s_ref, s_scr, *, n_full, tail):
    C = GLA_CHUNK
    s_scr[...] = s0_ref[0, 0]

    def do_chunk(r0, skip):
        sl = pl.ds(r0, C)
        q, k, la, v = q_ref[0, sl, :], k_ref[0, sl, :], la_ref[0, sl, :], v_ref[0, sl, :]
        if skip:
            live = lax.broadcasted_iota(jnp.int32, (C, GLA_DK), 0) >= skip
            live_v = lax.broadcasted_iota(jnp.int32, (C, GLA_DV), 0) >= skip
            q, k, la = jnp.where(live, q, 0.0), jnp.where(live, k, 0.0), jnp.where(live, la, 0.0)
            v = jnp.where(live_v, v, jnp.zeros_like(v))
        o, S_new = _gla_chunk(q, k, la, v, s_scr[...], C)
        if skip:
            o_ref[0, pl.ds(r0 + skip, C - skip), :] = o[skip:]
        else:
            o_ref[0, sl, :] = o
        s_scr[...] = S_new

    def body(i, carry):
        do_chunk(pl.multiple_of(i * C, C), 0)
        return carry

    lax.fori_loop(0, n_full, body, 0)
    if tail:
        do_chunk((n_full - 1) * C + tail, C - tail)
    s_ref[0, 0] = s_scr[...]


def _gla(q, k, la, v, s0):
    B, L, _ = q.shape
    n_full, tail = divmod(L, GLA_CHUNK)
    qk = pl.BlockSpec((1, L, GLA_DK), lambda b, h: (b, 0, h))
    vv = pl.BlockSpec((1, L, GLA_DV), lambda b, h: (b, 0, h))
    st = pl.BlockSpec((1, 1, GLA_DK, GLA_DV), lambda b, h: (b, h, 0, 0))
    return pl.pallas_call(
        functools.partial(_gla_kernel, n_full=n_full, tail=tail),
        grid=(B, GLA_HEADS),
        in_specs=[qk, qk, qk, vv, st],
        out_specs=[vv, st],
        out_shape=[jax.ShapeDtypeStruct((B, L, GLA_HEADS * GLA_DV), F32),
                   jax.ShapeDtypeStruct((B, GLA_HEADS, GLA_DK, GLA_DV), F32)],
        scratch_shapes=[pltpu.VMEM((GLA_DK, GLA_DV), F32)],
        compiler_params=pltpu.CompilerParams(dimension_semantics=("parallel", "parallel"),
                                             vmem_limit_bytes=VMEM_LIMIT),
        name="gla",
    )(q, k, la, v, s0)


def _stack_maps(q):
    lane = lax.broadcasted_iota(jnp.int32, q.shape, 1)
    zero = jnp.zeros_like(q)
    return jnp.concatenate([jnp.where(lane < HEAD_DIM, q, zero), jnp.where(lane >= HEAD_DIM, q, zero)], axis=0)


def _diff_prompt_kernel(lam_ref, q_ref, k_ref, v_ref, o_ref, k_scr, v_scr, *, L):
    lam = lam_ref[0]
    C = ATT_BLOCK
    Lpad = k_scr.shape[0]
    n_blocks = Lpad // C
    k_scr[0:L, :] = k_ref[0]
    v_scr[0:L, :] = v_ref[0]
    if Lpad > L:
        k_scr[L:Lpad, :] = jnp.zeros((Lpad - L, LANES), BF16)
        v_scr[L:Lpad, :] = jnp.zeros((Lpad - L, LANES), BF16)

    def q_block(r0, Lk):
        q2 = _stack_maps(q_ref[0, pl.ds(r0, C), :])
        s = _dot_nt(q2, k_scr[0:Lk, :])
        row = r0 + lax.broadcasted_iota(jnp.int32, s.shape, 0) % C
        col = lax.broadcasted_iota(jnp.int32, s.shape, 1)
        s = jnp.where(col <= row, s, -jnp.inf)
        p = jnp.exp(s - jnp.max(s, axis=1, keepdims=True))
        l = jnp.sum(p, axis=1, keepdims=True)
        o = _dot(p.astype(BF16), v_scr[0:Lk, :]) / l
        o_ref[0, pl.ds(r0, C), :] = o[:C] - lam * o[C:]

    for i0 in range(0, n_blocks, ATT_GROUP):
        i1 = min(i0 + ATT_GROUP, n_blocks)

        def body(i, carry, Lk=i1 * C):
            q_block(pl.multiple_of(jnp.minimum(i * C, L - C), 2 * SUBLANES), Lk)
            return carry

        lax.fori_loop(i0, i1, body, 0)


def _diff_prompt(lam, q, k, v):
    B, L, W = q.shape
    Lpad = -(-L // ATT_BLOCK) * ATT_BLOCK
    blk = pl.BlockSpec((1, L, LANES), lambda b, h, lam_ref: (b, 0, h))
    return pl.pallas_call(
        functools.partial(_diff_prompt_kernel, L=L),
        grid_spec=pltpu.PrefetchScalarGridSpec(
            num_scalar_prefetch=1, grid=(B, DIFF_HEADS), in_specs=[blk, blk, blk], out_specs=blk,
            scratch_shapes=[pltpu.VMEM((Lpad, LANES), BF16), pltpu.VMEM((Lpad, LANES), BF16)]),
        out_shape=jax.ShapeDtypeStruct((B, L, W), F32),
        compiler_params=pltpu.CompilerParams(dimension_semantics=("parallel", "parallel"),
                                             vmem_limit_bytes=VMEM_LIMIT),
        name="diff_prompt",
    )(lam, q, k, v)


def _diff_sample_kernel(pt_ref, lam_ref, q_ref, kn_ref, vn_ref, *refs, T):
    G = PAGES_PER_STEP
    k_refs, v_refs = refs[:G], refs[G:2 * G]
    o_ref, m_scr, l_scr, acc_scr = refs[2 * G:]
    p_id = pl.program_id(1)
    q = q_ref[0]
    rows_per_head = 2 * T
    page = k_refs[0].shape[2]

    @pl.when(p_id == 0)
    def _():
        s = _dot_nt(q, kn_ref[0])
        t_row = lax.broadcasted_iota(jnp.int32, s.shape, 0) % T
        t_col = lax.broadcasted_iota(jnp.int32, s.shape, 1)
        s = jnp.where(t_col <= t_row, s, -jnp.inf)
        m = jnp.max(s, axis=1, keepdims=True)
        p = jnp.exp(s - m)
        m_scr[...] = m
        l_scr[...] = jnp.sum(p, axis=1, keepdims=True)
        vn = vn_ref[0]
        acc_scr[...] = jnp.concatenate(
            [_dot(p[h * rows_per_head:(h + 1) * rows_per_head].astype(BF16), vn[:, h * LANES:(h + 1) * LANES])
             for h in range(DIFF_HEADS)], axis=0)

    s = jnp.concatenate([_dot(q, k_refs[g][0].astype(BF16)) for g in range(G)], axis=1)
    m_old = m_scr[...]
    m_new = jnp.maximum(m_old, jnp.max(s, axis=1, keepdims=True))
    p = jnp.exp(s - m_new)
    corr = jnp.exp(m_old - m_new)
    pv = []
    for h in range(DIFF_HEADS):
        ph = p[h * rows_per_head:(h + 1) * rows_per_head].astype(BF16)
        acc_h = None
        for g in range(G):
            vh = v_refs[g][0, pl.ds(h, page, stride=DIFF_HEADS), :].astype(BF16)
            d = _dot(ph[:, g * page:(g + 1) * page], vh)
            acc_h = d if acc_h is None else acc_h + d
        pv.append(acc_h)
    m_scr[...] = m_new
    l_scr[...] = l_scr[...] * corr + jnp.sum(p, axis=1, keepdims=True)
    acc_scr[...] = acc_scr[...] * corr + jnp.concatenate(pv, axis=0)

    @pl.when(p_id == pl.num_programs(1) - 1)
    def _():
        lam = lam_ref[0]
        o = acc_scr[...] / l_scr[...]
        pieces = []
        for h in range(DIFF_HEADS):
            tile = o[h * rows_per_head:(h + 1) * rows_per_head]
            pieces.append(tile[:T] - lam * tile[T:])
        o_ref[0] = jnp.concatenate(pieces, axis=1)


def _diff_sample(page_table, lam, q_rows, k_new, v_new, cache_kt, cache_v, T):
    Bd, R, W = q_rows.shape
    n_pages = page_table.shape[1]
    G = PAGES_PER_STEP
    page = cache_kt.shape[2]
    Tp = k_new.shape[1]
    per_b = lambda shape: pl.BlockSpec((1,) + shape, lambda b, p, pt, lam_ref: (b, 0, 0))
    kpage = lambda g: pl.BlockSpec((1, W, page), lambda b, p, pt, lam_ref, g=g: (pt[b, p * G + g], 0, 0))
    vpage = lambda g: pl.BlockSpec((1, page * DIFF_HEADS, LANES),
                                   lambda b, p, pt, lam_ref, g=g: (pt[b, p * G + g], 0, 0))
    return pl.pallas_call(
        functools.partial(_diff_sample_kernel, T=T),
        grid_spec=pltpu.PrefetchScalarGridSpec(
            num_scalar_prefetch=2, grid=(Bd, n_pages // G),
            in_specs=[per_b((R, W)), per_b((Tp, W)), per_b((Tp, W))]
                     + [kpage(g) for g in range(G)] + [vpage(g) for g in range(G)],
            out_specs=per_b((T, W)),
            scratch_shapes=[pltpu.VMEM((R, 1), F32), pltpu.VMEM((R, 1), F32), pltpu.VMEM((R, LANES), F32)]),
        out_shape=jax.ShapeDtypeStruct((Bd, T, W), F32),
        compiler_params=pltpu.CompilerParams(dimension_semantics=("parallel", "arbitrary"),
                                             vmem_limit_bytes=VMEM_LIMIT),
        name="diff_sample",
    )(page_table, lam, q_rows, k_new, v_new, *([cache_kt] * G), *([cache_v] * G))


def _head_rms(x, g, width):
    outs = []
    for h in range(x.shape[1] // width):
        xh = x[:, h * width:(h + 1) * width]
        outs.append(xh * lax.rsqrt(jnp.mean(xh * xh, axis=1, keepdims=True) + NORM_EPS) * g)
    return jnp.concatenate(outs, axis=1)


def _merge_kernel(x_ref, og_ref, gr_ref, od_ref, ma_ref, mb_ref, lng_ref, lnb_ref, gg_ref, dg_ref, wo_ref,
                  l1g_ref, l1b_ref, h1_ref):
    h = _layer_norm(x_ref[...], lng_ref[...], lnb_ref[...])
    a = _head_rms(og_ref[...], gg_ref[...], GLA_DV)
    a = a * (gr_ref[...] * _sigmoid(gr_ref[...]))
    d = _head_rms(od_ref[...], dg_ref[...], 2 * HEAD_DIM) * (1.0 - LAM_INIT)
    mixed = _sigmoid(ma_ref[...]) * a + _sigmoid(mb_ref[...]) * d
    h1_ref[...] = _layer_norm(DEEPNORM_ALPHA * h + _dot(mixed.astype(BF16), wo_ref[...]), l1g_ref[...],
                              l1b_ref[...])


def _merge(x, og, gr, od, ma, mb, vecs, wo, tm):
    R, D = x.shape
    row = pl.BlockSpec((tm, D), lambda i: (i, 0))
    return pl.pallas_call(
        _merge_kernel,
        grid=(R // tm,),
        in_specs=[row] * 6 + [_const_spec((1, D)), _const_spec((1, D)), _const_spec((1, GLA_DV)),
                              _const_spec((1, 2 * HEAD_DIM)), _const_spec(wo.shape), _const_spec((1, D)),
                              _const_spec((1, D))],
        out_specs=row,
        out_shape=jax.ShapeDtypeStruct((R, D), F32),
        compiler_params=pltpu.CompilerParams(dimension_semantics=("parallel",), vmem_limit_bytes=VMEM_LIMIT),
        name="merge",
    )(x, og, gr, od, ma, mb, *vecs[:4], wo, *vecs[4:6])


def _ffn_kernel(*refs, seq_len, has_state):
    (h1_ref, wug_ref, wuu_ref, bug_ref, buu_ref, cwg_ref, cwu_ref, cbg_ref, cbu_ref, wd_ref, l2g_ref,
     l2b_ref) = refs[:12]
    if has_state:
        p1g_ref, p1u_ref, p2g_ref, p2u_ref, y_ref, ug_ref, uu_ref, cg_scr, cu_scr = refs[12:]
    else:
        y_ref, ug_ref, uu_ref, cg_scr, cu_scr = refs[12:]
    tm = h1_ref.shape[0]
    W = wug_ref.shape[1]

    @pl.when(pl.program_id(0) == 0)
    def _():
        cg_scr[...] = jnp.zeros_like(cg_scr)
        cu_scr[...] = jnp.zeros_like(cu_scr)

    h1 = h1_ref[...]
    h1b = h1.astype(BF16)
    pos = (pl.program_id(0) * tm + lax.broadcasted_iota(jnp.int32, (tm, W), 0)) % seq_len
    row8 = lax.broadcasted_iota(jnp.int32, (SUBLANES, W), 0)

    def shifted(u, s, prev8, state):
        r = pltpu.roll(u, s, 0)
        head = jnp.where(row8 < s, pltpu.roll(prev8, s, 0), r[:SUBLANES])
        r = jnp.concatenate([head, r[SUBLANES:]], axis=0)
        return jnp.where(pos >= s, r, state)

    def conv(u, cw, cb, prev8, st1, st2):
        return cb + cw[0:1] * shifted(u, 2, prev8, st2) + cw[1:2] * shifted(u, 1, prev8, st1) + cw[2:3] * u

    u_g = _dot(h1b, wug_ref[...]) + bug_ref[...]
    u_u = _dot(h1b, wuu_ref[...]) + buu_ref[...]
    if has_state:
        st = (p1g_ref[...], p2g_ref[...], p1u_ref[...], p2u_ref[...])
        ug_ref[...] = u_g
        uu_ref[...] = u_u
    else:
        st = (0.0, 0.0, 0.0, 0.0)
        ug_ref[0] = u_g[tm - SUBLANES:]
        uu_ref[0] = u_u[tm - SUBLANES:]
    gate = conv(u_g, cwg_ref[...], cbg_ref[...], cg_scr[...], st[0], st[1])
    up = conv(u_u, cwu_ref[...], cbu_ref[...], cu_scr[...], st[2], st[3])
    cg_scr[...] = u_g[tm - SUBLANES:]
    cu_scr[...] = u_u[tm - SUBLANES:]
    gelu = 0.5 * gate * (1.0 + jnp.tanh(math.sqrt(2.0 / math.pi) * (gate + 0.044715 * (gate * gate * gate))))
    f = _dot((gelu * up).astype(BF16), wd_ref[...])
    y_ref[...] = _layer_norm(DEEPNORM_ALPHA * h1 + f, l2g_ref[...], l2b_ref[...])


def _ffn(h1, wug, wuu, bug, buu, cwg, cwu, cbg, cbu, wd, l2g, l2b, tm, seq_len, state=None):
    R, D = h1.shape
    W = wug.shape[1]
    n_tiles = R // tm
    has_state = state is not None
    row = pl.BlockSpec((tm, D), lambda i: (i, 0))
    wide = pl.BlockSpec((tm, W), lambda i: (i, 0))
    in_specs = [row] + [_const_spec(a.shape) for a in (wug, wuu, bug, buu, cwg, cwu, cbg, cbu, wd, l2g, l2b)]
    args = [h1, wug, wuu, bug, buu, cwg, cwu, cbg, cbu, wd, l2g, l2b]
    if has_state:
        in_specs += [wide] * 4
        args += list(state)
        u_spec, u_shape = wide, jax.ShapeDtypeStruct((R, W), F32)
    else:
        u_spec = pl.BlockSpec((1, SUBLANES, W), lambda i: (i, 0, 0))
        u_shape = jax.ShapeDtypeStruct((n_tiles, SUBLANES, W), F32)
    return pl.pallas_call(
        functools.partial(_ffn_kernel, seq_len=seq_len, has_state=has_state),
        grid=(n_tiles,),
        in_specs=in_specs,
        out_specs=[row, u_spec, u_spec],
        out_shape=[jax.ShapeDtypeStruct((R, D), F32), u_shape, u_shape],
        scratch_shapes=[pltpu.VMEM((SUBLANES, W), F32), pltpu.VMEM((SUBLANES, W), F32)],
        compiler_params=pltpu.CompilerParams(dimension_semantics=("arbitrary",), vmem_limit_bytes=VMEM_LIMIT),
        name="ffn_state" if has_state else "ffn",
    )(*args)


def _rope_tables(pos):
    inv_freq = 1.0 / (ROPE_THETA ** (jnp.arange(0, HEAD_DIM, 2, dtype=F32) / HEAD_DIM))
    ang = pos.astype(F32)[:, None] * inv_freq[None, :]
    ang = jnp.concatenate([ang, ang], axis=-1)
    sign = jnp.concatenate([-jnp.ones((HEAD_DIM // 2,), F32), jnp.ones((HEAD_DIM // 2,), F32)])
    return jnp.tile(jnp.cos(ang), (1, 2)), jnp.tile(jnp.sin(ang) * sign, (1, 2))


def _row_tile(L, target):
    best = SUBLANES
    for t in range(SUBLANES, target + 1, SUBLANES):
        if L % t == 0:
            best = t
    return best


def kernel(x_prompt, x_sample, cache_k, cache_v, state_gla, state_ffn_conv, page_table, meta_tokens, ln_in_g, ln_in_b, w_in, w_gate_up, b_gate, gla_norm_g, lambda_q1, lambda_k1, lambda_q2, lambda_k2, diff_norm_g, w_o, ln1_g, ln1_b, w_up, b_up, conv_w, conv_b, w_down, ln2_g, ln2_b):
    B, S, D = x_prompt.shape
    Bd, T, _ = x_sample.shape
    L = N_META + S
    n_phys, page_size = cache_k.shape[1], cache_k.shape[2]
    past = page_table.shape[1] * page_size
    d_ff = w_down.shape[1]
    l = 0

    wi = w_in[l]
    offs = [0, 512, 1024, 2048, 3072, 3088, 4112, 5136, 6160, 7184, 8208]
    seg = lambda i: wi[:, offs[i]:offs[i + 1]]
    w_main = jnp.concatenate([seg(0), seg(1), seg(2), seg(3), seg(5), seg(6), seg(7), seg(8), seg(9)],
                             axis=1).astype(BF16)
    w_ga = jnp.pad(seg(4), ((0, 0), (0, LANES - GLA_GATE_RANK))).astype(BF16)
    w_gu = jnp.pad(w_gate_up[l], ((0, LANES - GLA_GATE_RANK), (0, 0))).astype(BF16)
    lam = (jnp.exp(jnp.sum(lambda_q1[l] * lambda_k1[l])) - jnp.exp(jnp.sum(lambda_q2[l] * lambda_k2[l]))
           + LAM_INIT).reshape(1).astype(F32)
    r1 = lambda v: v.reshape(1, -1)
    vecs = [r1(ln_in_g), r1(ln_in_b), r1(gla_norm_g[l]), r1(diff_norm_g[l]), r1(ln1_g[l]), r1(ln1_b[l]),
            r1(ln2_g[l]), r1(ln2_b[l])]
    wo = w_o[l].astype(BF16)
    ffn_w = (w_up[l][:, :d_ff].astype(BF16), w_up[l][:, d_ff:].astype(BF16), b_up[l][None, :d_ff],
             b_up[l][None, d_ff:], conv_w[l][:, :d_ff], conv_w[l][:, d_ff:], conv_b[l][None, :d_ff],
             conv_b[l][None, d_ff:], w_down[l].astype(BF16), vecs[6], vecs[7])

    xp = jnp.concatenate([jnp.broadcast_to(meta_tokens[None], (B, N_META, D)), x_prompt], axis=1).reshape(B * L, D)
    tm = _row_tile(L, 344)
    cos_p, sin_p = _rope_tables(jnp.arange(L))
    gq, gk, la, gr, ma, mb, kf, vf, gv, qb, kb, vb = _inproj(xp, ln_in_g, ln_in_b, cos_p, sin_p, w_main, w_ga, w_gu,
                                                            b_gate[l], tm)
    b3 = lambda a: a.reshape(B, L, a.shape[-1])
    o_gla, s_p = _gla(b3(gq), b3(gk), b3(la), b3(gv), jnp.zeros((B, GLA_HEADS, GLA_DK, GLA_DV), F32))
    o_diff = _diff_prompt(lam, b3(qb), b3(kb), b3(vb))
    h1 = _merge(xp, o_gla.reshape(B * L, D), gr, o_diff.reshape(B * L, D), ma, mb, vecs, wo, tm)
    y_full, utg, utu = _ffn(h1, *ffn_w, tm=tm, seq_len=L)
    y_prompt = y_full.reshape(B, L, D)[:, N_META:]
    tiles_per_seq = L // tm
    last = lambda u: u.reshape(B, tiles_per_seq, SUBLANES, d_ff)[:, -1, SUBLANES - (CONV_WIDTH - 1):, :]
    conv_p = jnp.concatenate([last(utg), last(utu)], axis=-1)
    k_prompt = kf.reshape(1, B, L, 2 * DIFF_HEADS, HEAD_DIM)
    v_prompt = vf.reshape(1, B, L, DIFF_HEADS, 2 * HEAD_DIM)

    Rs = Bd * T
    xs = x_sample.reshape(Rs, D)
    cos_s, sin_s = _rope_tables(past + jnp.arange(T))
    cos_s, sin_s = jnp.tile(cos_s, (Bd, 1)), jnp.tile(sin_s, (Bd, 1))
    sq, sk, sa, sr, na, nb, ekf, evf, sv, eqb, ekb, evb = _inproj(xs, ln_in_g, ln_in_b, cos_s, sin_s, w_main, w_ga,
                                                                  w_gu, b_gate[l], Rs)
    Tp = GLA_CHUNK
    padT = lambda a: jnp.pad(a.reshape(Bd, T, a.shape[-1]), ((0, 0), (Tp - T, 0), (0, 0)))
    o_gla_s, s_s = _gla(padT(sq), padT(sk), padT(sa), padT(sv), state_gla[l])
    o_gla_s = o_gla_s[:, Tp - T:].reshape(Rs, D)
    q4 = eqb.reshape(Bd, T, 2 * DIFF_HEADS, HEAD_DIM)
    eye = jnp.eye(2 * DIFF_HEADS, dtype=BF16)
    q_rows = (q4.transpose(0, 2, 1, 3)[:, :, :, None, :] * eye[None, :, None, :, None]).reshape(
        Bd, 2 * DIFF_HEADS * T, D)
    Tk = 2 * SUBLANES
    padK = lambda a: jnp.pad(a.reshape(Bd, T, D), ((0, 0), (0, Tk - T), (0, 0)))
    cache_kt = cache_k[l].transpose(0, 2, 3, 1).reshape(n_phys, D, page_size)
    cache_vr = cache_v[l].reshape(n_phys, page_size * DIFF_HEADS, 2 * HEAD_DIM)
    o_diff_s = _diff_sample(page_table, lam, q_rows, padK(ekb), padK(evb), cache_kt, cache_vr, T)
    st = state_ffn_conv[l]
    zero = jnp.zeros((Bd, 1, 2 * d_ff), F32)
    prev1 = jnp.concatenate([st[:, 1:2]] + [zero] * (T - 1), axis=1).reshape(Rs, 2 * d_ff)
    prev2 = jnp.concatenate([st[:, 0:1], st[:, 1:2]] + [zero] * (T - 2), axis=1).reshape(Rs, 2 * d_ff)
    state = (prev1[:, :d_ff], prev1[:, d_ff:], prev2[:, :d_ff], prev2[:, d_ff:])
    h1_s = _merge(xs, o_gla_s, sr, o_diff_s.reshape(Rs, D), na, nb, vecs, wo, Rs)
    y_s, usg, usu = _ffn(h1_s, *ffn_w, tm=Rs, seq_len=T, state=state)
    y_sample = y_s.reshape(Bd, T, D)
    new_rows = lambda u: u.reshape(Bd, T, d_ff)[:, T - (CONV_WIDTH - 1):]
    conv_s = jnp.concatenate([new_rows(usg), new_rows(usu)], axis=-1)
    k_sample = ekf.reshape(1, Bd, T, 2 * DIFF_HEADS, HEAD_DIM)
    v_sample = evf.reshape(1, Bd, T, DIFF_HEADS, 2 * HEAD_DIM)

    return (y_prompt, y_sample, k_prompt, v_prompt, s_p[None], conv_p[None], k_sample, v_sample, s_s[None],
            conv_s[None])
```

```python
import functools
import math

import jax
import jax.numpy as jnp
from jax import lax
from jax.experimental import pallas as pl
from jax.experimental.pallas import tpu as pltpu

F32 = jnp.float32
BF16 = jnp.bfloat16

N_META = 16
GLA_HEADS = 4
GLA_DK = 128
GLA_DV = 256
GLA_GATE_RANK = 16
GLA_GATE_TAU = 16.0
HEAD_DIM = 64
DIFF_HEADS = 8
ROPE_THETA = 10000.0
NORM_EPS = 1e-5
CONV_WIDTH = 3
DEPTH = 1
DEEPNORM_ALPHA = (2.0 * DEPTH) ** 0.25
LAM_INIT = 0.8 - 0.6 * math.exp(-0.3 * 0)
LOG2E = math.log2(math.e)

LANES = 128
SUBLANES = 8
VMEM_LIMIT = 56 * 1024 * 1024
GLA_CHUNK = 128
GLA_DIAG = 8
GLA_HEADS_PER_STEP = 2
ATT_BLOCK = 128
ATT_GROUP = 4
PAGES_PER_STEP = 8


def _const_spec(shape):
    return pl.BlockSpec(shape, lambda *_: (0,) * len(shape), pipeline_mode=pl.Buffered(1))


def _layer_norm(x, g, b):
    mu = jnp.mean(x, axis=-1, keepdims=True)
    xc = x - mu
    var = jnp.mean(xc * xc, axis=-1, keepdims=True)
    return xc * lax.rsqrt(var + NORM_EPS) * g + b


def _sigmoid(x):
    return 1.0 / (1.0 + jnp.exp(-x))


def _dot(a, b):
    return jnp.dot(a, b, preferred_element_type=F32)


def _dot_nt(a, b):
    return lax.dot_general(a, b, (((1,), (1,)), ((), ())), preferred_element_type=F32)


def _inproj_kernel(x_ref, g_ref, b_ref, cos_ref, sin_ref, w_ref, wga_ref, wgu_ref, bg_ref,
                   gq_o, gk_o, la_o, gr_o, ma_o, mb_o, kf_o, vf_o, gv_o, q_o, k_o, v_o):
    h = _layer_norm(x_ref[...], g_ref[...], b_ref[...])
    hb = h.astype(BF16)

    def seg(c0, n):
        return _dot(hb, w_ref[:, c0:c0 + n])

    gq_o[...] = seg(0, 512) * (GLA_DK ** -0.5)
    gk_o[...] = seg(512, 512)
    gv_o[...] = seg(1024, 1024).astype(BF16)
    gr_o[...] = seg(2048, 1024)

    ga = _dot(hb, wga_ref[...])
    xg = _dot(ga.astype(BF16), wgu_ref[...]) + bg_ref[...]
    la_o[...] = (jnp.minimum(xg, 0.0) - jnp.log1p(jnp.exp(-jnp.abs(xg)))) * (1.0 / GLA_GATE_TAU)

    cos = cos_ref[...]
    sin = sin_ref[...]
    lane = lax.broadcasted_iota(jnp.int32, cos.shape, 1)
    first_half = (lane % HEAD_DIM) < (HEAD_DIM // 2)

    def rope(z):
        outs = []
        for g in range(z.shape[1] // LANES):
            zg = z[:, g * LANES:(g + 1) * LANES]
            rot = jnp.where(first_half, pltpu.roll(zg, LANES - HEAD_DIM // 2, 1), pltpu.roll(zg, HEAD_DIM // 2, 1))
            outs.append(zg * cos + rot * sin)
        return jnp.concatenate(outs, axis=1)

    q_o[...] = (rope(seg(3072, 1024)) * (HEAD_DIM ** -0.5 * LOG2E)).astype(BF16)
    kr = rope(seg(4096, 1024))
    kf_o[...] = kr
    k_o[...] = kr.astype(BF16)
    vv = seg(5120, 1024)
    vf_o[...] = vv
    v_o[...] = vv.astype(BF16)
    ma_o[...] = seg(6144, 1024)
    mb_o[...] = seg(7168, 1024)


def _inproj(x, ln_g, ln_b, cos_t, sin_t, w_main, w_ga, w_gu, b_gate, tm):
    R, D = x.shape
    period = cos_t.shape[0] // tm
    row = lambda w: pl.BlockSpec((tm, w), lambda i: (i, 0))
    tab = pl.BlockSpec((tm, LANES), lambda i: (i % period, 0))
    f32_out = lambda w: jax.ShapeDtypeStruct((R, w), F32)
    bf_out = lambda w: jax.ShapeDtypeStruct((R, w), BF16)
    return pl.pallas_call(
        _inproj_kernel,
        grid=(R // tm,),
        in_specs=[row(D), _const_spec((1, D)), _const_spec((1, D)), tab, tab,
                  _const_spec(w_main.shape), _const_spec(w_ga.shape), _const_spec(w_gu.shape),
                  _const_spec((1, 512))],
        out_specs=[row(512), row(512), row(512), row(1024), row(1024), row(1024), row(1024), row(1024),
                   row(1024), row(1024), row(1024), row(1024)],
        out_shape=[f32_out(512), f32_out(512), f32_out(512), f32_out(1024), f32_out(1024), f32_out(1024),
                   f32_out(1024), f32_out(1024), bf_out(1024), bf_out(1024), bf_out(1024), bf_out(1024)],
        compiler_params=pltpu.CompilerParams(dimension_semantics=("parallel",), vmem_limit_bytes=VMEM_LIMIT),
        name="inproj",
    )(x, ln_g.reshape(1, D), ln_b.reshape(1, D), cos_t, sin_t, w_main, w_ga, w_gu, b_gate.reshape(1, 512))


def _split3(x):
    hi = x.astype(BF16)
    r1 = x - hi.astype(F32)
    mid = r1.astype(BF16)
    lo = (r1 - mid.astype(F32)).astype(BF16)
    return hi, mid, lo


def _gla_consts(C):
    row = lax.broadcasted_iota(jnp.int32, (C, C), 0)
    col = lax.broadcasted_iota(jnp.int32, (C, C), 1)
    rowk = lax.broadcasted_iota(jnp.int32, (C, GLA_DK), 0)
    tri = jnp.where(row >= col, 1.0, 0.0).astype(BF16)
    levels = []
    hs = C // 2
    while hs >= GLA_DIAG:
        grp = 2 * hs
        upper = (rowk % grp) >= hs
        keep = ((row // grp) == (col // grp)) & ((row % grp) >= hs) & ((col % grp) < hs)
        levels.append((hs, jnp.where(upper, 0.0, -jnp.inf), jnp.where(upper, -jnp.inf, 0.0),
                       jnp.where(keep, 1.0, 0.0)))
        hs //= 2
    d = GLA_DIAG
    diag = [(jnp.where((rowk % d) >= jj, 0.0, -jnp.inf), jnp.where(col == (row // d) * d + jj, 1.0, 0.0))
            for jj in range(d)]
    return tri, levels, diag


def _gla_intra(q, k, bc, C, levels, diag):
    att = jnp.zeros((C, C), F32)
    for hs, q_bias, k_bias, keep in levels:
        grp = 2 * hs
        refs = jnp.concatenate(
            [jnp.broadcast_to(bc[g * grp + hs - 1:g * grp + hs, :], (grp, GLA_DK)) for g in range(C // grp)], axis=0)
        qs = (q * jnp.exp2((bc - refs) + q_bias)).astype(BF16)
        ks = (k * jnp.exp2((refs - bc) + k_bias)).astype(BF16)
        att = att + _dot_nt(qs, ks) * keep
    d = GLA_DIAG
    nb = C // d
    k3 = k.reshape(nb, d, GLA_DK)
    b3 = bc.reshape(nb, d, GLA_DK)
    for jj, (bias, place) in enumerate(diag):
        kb = jnp.broadcast_to(k3[:, jj:jj + 1, :], (nb, d, GLA_DK)).reshape(C, GLA_DK)
        bb = jnp.broadcast_to(b3[:, jj:jj + 1, :], (nb, d, GLA_DK)).reshape(C, GLA_DK)
        colsum = jnp.sum(q * kb * jnp.exp2((bc - bb) + bias), axis=1, keepdims=True)
        att = att + colsum * place
    return att


def _gla_chunk(q, k, la, v, S, C, consts):
    tri, levels, diag = consts
    hi, mid, lo = _split3(la * LOG2E)
    bc = _dot(tri, hi) + _dot(tri, mid) + _dot(tri, lo)
    o = _dot((q * jnp.exp2(bc)).astype(BF16), S.astype(BF16))
    att = _gla_intra(q, k, bc, C, levels, diag)
    o = o + _dot(att.astype(BF16), v)
    bct = bc.T
    blt = bct[:, C - 1:C]
    kdt = (k.T * jnp.exp2(blt - bct)).astype(BF16)
    S_new = jnp.exp2(blt) * S + _dot(kdt, v)
    return o, S_new


def _gla_kernel(q_ref, k_ref, la_ref, v_ref, s0_ref, o_ref, s_ref, s_scr, *, n_full, tail):
    C = GLA_CHUNK
    consts = _gla_consts(C)
    s_scr[...] = s0_ref[0]

    def do_chunk(r0, skip):
        sl = pl.ds(r0, C)
        for h in range(GLA_HEADS_PER_STEP):
            kc = slice(h * GLA_DK, (h + 1) * GLA_DK)
            vc = slice(h * GLA_DV, (h + 1) * GLA_DV)
            q, k, la, v = q_ref[0, sl, kc], k_ref[0, sl, kc], la_ref[0, sl, kc], v_ref[0, sl, vc]
            if skip:
                live = lax.broadcasted_iota(jnp.int32, (C, GLA_DK), 0) >= skip
                live_v = lax.broadcasted_iota(jnp.int32, (C, GLA_DV), 0) >= skip
                q, k, la = jnp.where(live, q, 0.0), jnp.where(live, k, 0.0), jnp.where(live, la, 0.0)
                v = jnp.where(live_v, v, jnp.zeros_like(v))
            o, S_new = _gla_chunk(q, k, la, v, s_scr[h], C, consts)
            if skip:
                o_ref[0, pl.ds(r0 + skip, C - skip), vc] = o[skip:]
            else:
                o_ref[0, sl, vc] = o
            s_scr[h] = S_new

    def body(i, carry):
        do_chunk(pl.multiple_of(i * C, C), 0)
        return carry

    lax.fori_loop(0, n_full, body, 0)
    if tail:
        do_chunk((n_full - 1) * C + tail, C - tail)
    s_ref[0] = s_scr[...]


def _gla(q, k, la, v, s0):
    B, L, _ = q.shape
    n_full, tail = divmod(L, GLA_CHUNK)
    HP = GLA_HEADS_PER_STEP
    qk = pl.BlockSpec((1, L, HP * GLA_DK), lambda b, h: (b, 0, h))
    vv = pl.BlockSpec((1, L, HP * GLA_DV), lambda b, h: (b, 0, h))
    st = pl.BlockSpec((1, HP, GLA_DK, GLA_DV), lambda b, h: (b, h, 0, 0))
    return pl.pallas_call(
        functools.partial(_gla_kernel, n_full=n_full, tail=tail),
        grid=(B, GLA_HEADS // HP),
        in_specs=[qk, qk, qk, vv, st],
        out_specs=[vv, st],
        out_shape=[jax.ShapeDtypeStruct((B, L, GLA_HEADS * GLA_DV), F32),
                   jax.ShapeDtypeStruct((B, GLA_HEADS, GLA_DK, GLA_DV), F32)],
        scratch_shapes=[pltpu.VMEM((HP, GLA_DK, GLA_DV), F32)],
        compiler_params=pltpu.CompilerParams(dimension_semantics=("parallel", "parallel"),
                                             vmem_limit_bytes=VMEM_LIMIT),
        name="gla",
    )(q, k, la, v, s0)


def _stack_maps(q):
    lane = lax.broadcasted_iota(jnp.int32, q.shape, 1)
    zero = jnp.zeros_like(q)
    return jnp.concatenate([jnp.where(lane < HEAD_DIM, q, zero), jnp.where(lane >= HEAD_DIM, q, zero)], axis=0)


def _diff_prompt_kernel(lam_ref, q_ref, k_ref, v_ref, o_ref, k_scr, v_scr, *, L):
    lam = lam_ref[0]
    C = ATT_BLOCK
    Lpad = k_scr.shape[0]
    n_blocks = Lpad // C
    k_scr[0:L, :] = k_ref[0]
    v_scr[0:L, 0:LANES] = v_ref[0]
    if Lpad > L:
        k_scr[L:Lpad, :] = jnp.zeros((Lpad - L, LANES), BF16)
        v_scr[L:Lpad, 0:LANES] = jnp.zeros((Lpad - L, LANES), BF16)
    v_scr[:, LANES:2 * LANES] = jnp.ones((Lpad, LANES), BF16)

    def q_block(r0, Lk, band):
        q2 = _stack_maps(q_ref[0, pl.ds(r0, C), :])
        s = _dot_nt(q2, k_scr[0:Lk, :])
        sb = s[:, Lk - band:]
        row = r0 + lax.broadcasted_iota(jnp.int32, sb.shape, 0) % C
        col = (Lk - band) + lax.broadcasted_iota(jnp.int32, sb.shape, 1)
        sb = jnp.where(col <= row, sb, -jnp.inf)
        m = jnp.max(sb, axis=1, keepdims=True)
        if Lk > band:
            sa = s[:, :Lk - band]
            m = jnp.maximum(m, jnp.max(sa, axis=1, keepdims=True))
            p = jnp.concatenate([jnp.exp2(sa - m).astype(BF16), jnp.exp2(sb - m).astype(BF16)], axis=1)
        else:
            p = jnp.exp2(sb - m).astype(BF16)
        ov = _dot(p, v_scr[0:Lk, :])
        o = ov[:, :LANES] / ov[:, LANES:]
        o_ref[0, pl.ds(r0, C), :] = o[:C] - lam * o[C:]

    for i0 in range(0, n_blocks, ATT_GROUP):
        i1 = min(i0 + ATT_GROUP, n_blocks)
        first_row = (min(i0 * C, L - C) // C) * C

        def body(i, carry, Lk=i1 * C, band=i1 * C - first_row):
            q_block(pl.multiple_of(jnp.minimum(i * C, L - C), 2 * SUBLANES), Lk, band)
            return carry

        lax.fori_loop(i0, i1, body, 0, unroll=2 if (i1 - i0) % 2 == 0 else 1)


def _diff_prompt(lam, q, k, v):
    B, L, W = q.shape
    Lpad = -(-L // ATT_BLOCK) * ATT_BLOCK
    blk = pl.BlockSpec((1, L, LANES), lambda b, h, lam_ref: (b, 0, h))
    return pl.pallas_call(
        functools.partial(_diff_prompt_kernel, L=L),
        grid_spec=pltpu.PrefetchScalarGridSpec(
            num_scalar_prefetch=1, grid=(B, DIFF_HEADS), in_specs=[blk, blk, blk], out_specs=blk,
            scratch_shapes=[pltpu.VMEM((Lpad, LANES), BF16), pltpu.VMEM((Lpad, 2 * LANES), BF16)]),
        out_shape=jax.ShapeDtypeStruct((B, L, W), F32),
        compiler_params=pltpu.CompilerParams(dimension_semantics=("parallel", "parallel"),
                                             vmem_limit_bytes=VMEM_LIMIT),
        name="diff_prompt",
    )(lam, q, k, v)


def _diff_sample_kernel(pt_ref, lam_ref, q_ref, kn_ref, vn_ref, *refs, T):
    G = PAGES_PER_STEP
    k_refs, v_refs = refs[:G], refs[G:2 * G]
    o_ref, m_scr, l_scr, acc_scr = refs[2 * G:]
    p_id = pl.program_id(1)
    q = q_ref[0]
    rows_per_head = 2 * T
    page = k_refs[0].shape[2]

    @pl.when(p_id == 0)
    def _():
        s = _dot_nt(q, kn_ref[0])
        t_row = lax.broadcasted_iota(jnp.int32, s.shape, 0) % T
        t_col = lax.broadcasted_iota(jnp.int32, s.shape, 1)
        s = jnp.where(t_col <= t_row, s, -jnp.inf)
        m = jnp.max(s, axis=1, keepdims=True)
        p = jnp.exp2(s - m)
        m_scr[...] = m
        l_scr[...] = jnp.sum(p, axis=1, keepdims=True)
        vn = vn_ref[0]
        acc_scr[...] = jnp.concatenate(
            [_dot(p[h * rows_per_head:(h + 1) * rows_per_head].astype(BF16), vn[:, h * LANES:(h + 1) * LANES])
             for h in range(DIFF_HEADS)], axis=0)

    s = jnp.concatenate([_dot(q, k_refs[g][0].astype(BF16)) for g in range(G)], axis=1)
    m_old = m_scr[...]
    m_new = jnp.maximum(m_old, jnp.max(s, axis=1, keepdims=True))
    p = jnp.exp2(s - m_new)
    corr = jnp.exp2(m_old - m_new)
    pv = []
    for h in range(DIFF_HEADS):
        ph = p[h * rows_per_head:(h + 1) * rows_per_head].astype(BF16)
        vh = jnp.concatenate([v_refs[g][0, pl.ds(h, page, stride=DIFF_HEADS), :].astype(BF16) for g in range(G)],
                             axis=0)
        pv.append(_dot(ph, vh))
    m_scr[...] = m_new
    l_scr[...] = l_scr[...] * corr + jnp.sum(p, axis=1, keepdims=True)
    acc_scr[...] = acc_scr[...] * corr + jnp.concatenate(pv, axis=0)

    @pl.when(p_id == pl.num_programs(1) - 1)
    def _():
        lam = lam_ref[0]
        o = acc_scr[...] / l_scr[...]
        pieces = []
        for h in range(DIFF_HEADS):
            tile = o[h * rows_per_head:(h + 1) * rows_per_head]
            pieces.append(tile[:T] - lam * tile[T:])
        o_ref[0] = jnp.concatenate(pieces, axis=1)


def _diff_sample(page_table, lam, q_rows, k_new, v_new, cache_kt, cache_v, T):
    Bd, R, W = q_rows.shape
    n_pages = page_table.shape[1]
    G = PAGES_PER_STEP
    page = cache_kt.shape[2]
    Tp = k_new.shape[1]
    per_b = lambda shape: pl.BlockSpec((1,) + shape, lambda b, p, pt, lam_ref: (b, 0, 0))
    kpage = lambda g: pl.BlockSpec((1, W, page), lambda b, p, pt, lam_ref, g=g: (pt[b, p * G + g], 0, 0))
    vpage = lambda g: pl.BlockSpec((1, page * DIFF_HEADS, LANES),
                                   lambda b, p, pt, lam_ref, g=g: (pt[b, p * G + g], 0, 0))
    return pl.pallas_call(
        functools.partial(_diff_sample_kernel, T=T),
        grid_spec=pltpu.PrefetchScalarGridSpec(
            num_scalar_prefetch=2, grid=(Bd, n_pages // G),
            in_specs=[per_b((R, W)), per_b((Tp, W)), per_b((Tp, W))]
                     + [kpage(g) for g in range(G)] + [vpage(g) for g in range(G)],
            out_specs=per_b((T, W)),
            scratch_shapes=[pltpu.VMEM((R, 1), F32), pltpu.VMEM((R, 1), F32), pltpu.VMEM((R, LANES), F32)]),
        out_shape=jax.ShapeDtypeStruct((Bd, T, W), F32),
        compiler_params=pltpu.CompilerParams(dimension_semantics=("parallel", "arbitrary"),
                                             vmem_limit_bytes=VMEM_LIMIT),
        name="diff_sample",
    )(page_table, lam, q_rows, k_new, v_new, *([cache_kt] * G), *([cache_v] * G))


def _head_rms(x, g, width):
    outs = []
    for h in range(x.shape[1] // width):
        xh = x[:, h * width:(h + 1) * width]
        outs.append(xh * lax.rsqrt(jnp.mean(xh * xh, axis=1, keepdims=True) + NORM_EPS) * g)
    return jnp.concatenate(outs, axis=1)


def _merge_kernel(x_ref, og_ref, gr_ref, od_ref, ma_ref, mb_ref, lng_ref, lnb_ref, gg_ref, dg_ref, wo_ref,
                  l1g_ref, l1b_ref, h1_ref):
    h = _layer_norm(x_ref[...], lng_ref[...], lnb_ref[...])
    a = _head_rms(og_ref[...], gg_ref[...], GLA_DV)
    a = a * (gr_ref[...] * _sigmoid(gr_ref[...]))
    d = _head_rms(od_ref[...], dg_ref[...], 2 * HEAD_DIM) * (1.0 - LAM_INIT)
    mixed = _sigmoid(ma_ref[...]) * a + _sigmoid(mb_ref[...]) * d
    h1_ref[...] = _layer_norm(DEEPNORM_ALPHA * h + _dot(mixed.astype(BF16), wo_ref[...]), l1g_ref[...],
                              l1b_ref[...])


def _merge(x, og, gr, od, ma, mb, vecs, wo, tm):
    R, D = x.shape
    row = pl.BlockSpec((tm, D), lambda i: (i, 0))
    return pl.pallas_call(
        _merge_kernel,
        grid=(R // tm,),
        in_specs=[row] * 6 + [_const_spec((1, D)), _const_spec((1, D)), _const_spec((1, GLA_DV)),
                              _const_spec((1, 2 * HEAD_DIM)), _const_spec(wo.shape), _const_spec((1, D)),
                              _const_spec((1, D))],
        out_specs=row,
        out_shape=jax.ShapeDtypeStruct((R, D), F32),
        compiler_params=pltpu.CompilerParams(dimension_semantics=("parallel",), vmem_limit_bytes=VMEM_LIMIT),
        name="merge",
    )(x, og, gr, od, ma, mb, *vecs[:4], wo, *vecs[4:6])


def _ffn_kernel(*refs, seq_len, has_state):
    (h1_ref, wug_ref, wuu_ref, bug_ref, buu_ref, cwg_ref, cwu_ref, cbg_ref, cbu_ref, wd_ref, l2g_ref,
     l2b_ref) = refs[:12]
    if has_state:
        p1g_ref, p1u_ref, p2g_ref, p2u_ref, y_ref, ug_ref, uu_ref, cg_scr, cu_scr = refs[12:]
    else:
        y_ref, ug_ref, uu_ref, cg_scr, cu_scr = refs[12:]
    tm = h1_ref.shape[0]
    W = wug_ref.shape[1]

    @pl.when(pl.program_id(0) == 0)
    def _():
        cg_scr[...] = jnp.zeros_like(cg_scr)
        cu_scr[...] = jnp.zeros_like(cu_scr)

    h1 = h1_ref[...]
    h1b = h1.astype(BF16)
    pos = (pl.program_id(0) * tm + lax.broadcasted_iota(jnp.int32, (tm, W), 0)) % seq_len
    row8 = lax.broadcasted_iota(jnp.int32, (SUBLANES, W), 0)

    def shifted(u, s, prev8, state):
        r = pltpu.roll(u, s, 0)
        head = jnp.where(row8 < s, pltpu.roll(prev8, s, 0), r[:SUBLANES])
        r = jnp.concatenate([head, r[SUBLANES:]], axis=0)
        return jnp.where(pos >= s, r, state)

    def conv(u, cw, cb, prev8, st1, st2):
        return cb + cw[0:1] * shifted(u, 2, prev8, st2) + cw[1:2] * shifted(u, 1, prev8, st1) + cw[2:3] * u

    u_g = _dot(h1b, wug_ref[...]) + bug_ref[...]
    u_u = _dot(h1b, wuu_ref[...]) + buu_ref[...]
    if has_state:
        st = (p1g_ref[...], p2g_ref[...], p1u_ref[...], p2u_ref[...])
        ug_ref[...] = u_g
        uu_ref[...] = u_u
    else:
        st = (0.0, 0.0, 0.0, 0.0)
        ug_ref[0] = u_g[tm - SUBLANES:]
        uu_ref[0] = u_u[tm - SUBLANES:]
    gate = conv(u_g, cwg_ref[...], cbg_ref[...], cg_scr[...], st[0], st[1])
    up = conv(u_u, cwu_ref[...], cbu_ref[...], cu_scr[...], st[2], st[3])
    cg_scr[...] = u_g[tm - SUBLANES:]
    cu_scr[...] = u_u[tm - SUBLANES:]
    gelu = 0.5 * gate * (1.0 + jnp.tanh(math.sqrt(2.0 / math.pi) * (gate + 0.044715 * (gate * gate * gate))))
    f = _dot((gelu * up).astype(BF16), wd_ref[...])
    y_ref[...] = _layer_norm(DEEPNORM_ALPHA * h1 + f, l2g_ref[...], l2b_ref[...])


def _ffn(h1, wug, wuu, bug, buu, cwg, cwu, cbg, cbu, wd, l2g, l2b, tm, seq_len, state=None):
    R, D = h1.shape
    W = wug.shape[1]
    n_tiles = R // tm
    has_state = state is not None
    row = pl.BlockSpec((tm, D), lambda i: (i, 0))
    wide = pl.BlockSpec((tm, W), lambda i: (i, 0))
    in_specs = [row] + [_const_spec(a.shape) for a in (wug, wuu, bug, buu, cwg, cwu, cbg, cbu, wd, l2g, l2b)]
    args = [h1, wug, wuu, bug, buu, cwg, cwu, cbg, cbu, wd, l2g, l2b]
    if has_state:
        in_specs += [wide] * 4
        args += list(state)
        u_spec, u_shape = wide, jax.ShapeDtypeStruct((R, W), F32)
    else:
        u_spec = pl.BlockSpec((1, SUBLANES, W), lambda i: (i, 0, 0))
        u_shape = jax.ShapeDtypeStruct((n_tiles, SUBLANES, W), F32)
    return pl.pallas_call(
        functools.partial(_ffn_kernel, seq_len=seq_len, has_state=has_state),
        grid=(n_tiles,),
        in_specs=in_specs,
        out_specs=[row, u_spec, u_spec],
        out_shape=[jax.ShapeDtypeStruct((R, D), F32), u_shape, u_shape],
        scratch_shapes=[pltpu.VMEM((SUBLANES, W), F32), pltpu.VMEM((SUBLANES, W), F32)],
        compiler_params=pltpu.CompilerParams(dimension_semantics=("arbitrary",), vmem_limit_bytes=VMEM_LIMIT),
        name="ffn_state" if has_state else "ffn",
    )(*args)


def _rope_tables(pos):
    inv_freq = 1.0 / (ROPE_THETA ** (jnp.arange(0, HEAD_DIM, 2, dtype=F32) / HEAD_DIM))
    ang = pos.astype(F32)[:, None] * inv_freq[None, :]
    ang = jnp.concatenate([ang, ang], axis=-1)
    sign = jnp.concatenate([-jnp.ones((HEAD_DIM // 2,), F32), jnp.ones((HEAD_DIM // 2,), F32)])
    return jnp.tile(jnp.cos(ang), (1, 2)), jnp.tile(jnp.sin(ang) * sign, (1, 2))


def _row_tile(L, target):
    best = SUBLANES
    for t in range(SUBLANES, target + 1, SUBLANES):
        if L % t == 0:
            best = t
    return best


def kernel(x_prompt, x_sample, cache_k, cache_v, state_gla, state_ffn_conv, page_table, meta_tokens, ln_in_g, ln_in_b, w_in, w_gate_up, b_gate, gla_norm_g, lambda_q1, lambda_k1, lambda_q2, lambda_k2, diff_norm_g, w_o, ln1_g, ln1_b, w_up, b_up, conv_w, conv_b, w_down, ln2_g, ln2_b):
    B, S, D = x_prompt.shape
    Bd, T, _ = x_sample.shape
    L = N_META + S
    n_phys, page_size = cache_k.shape[1], cache_k.shape[2]
    past = page_table.shape[1] * page_size
    d_ff = w_down.shape[1]
    l = 0

    wi = w_in[l]
    offs = [0, 512, 1024, 2048, 3072, 3088, 4112, 5136, 6160, 7184, 8208]
    seg = lambda i: wi[:, offs[i]:offs[i + 1]]
    w_main = jnp.concatenate([seg(0), seg(1), seg(2), seg(3), seg(5), seg(6), seg(7), seg(8), seg(9)],
                             axis=1).astype(BF16)
    w_ga = jnp.pad(seg(4), ((0, 0), (0, LANES - GLA_GATE_RANK))).astype(BF16)
    w_gu = jnp.pad(w_gate_up[l], ((0, LANES - GLA_GATE_RANK), (0, 0))).astype(BF16)
    lam = (jnp.exp(jnp.sum(lambda_q1[l] * lambda_k1[l])) - jnp.exp(jnp.sum(lambda_q2[l] * lambda_k2[l]))
           + LAM_INIT).reshape(1).astype(F32)
    r1 = lambda v: v.reshape(1, -1)
    vecs = [r1(ln_in_g), r1(ln_in_b), r1(gla_norm_g[l]), r1(diff_norm_g[l]), r1(ln1_g[l]), r1(ln1_b[l]),
            r1(ln2_g[l]), r1(ln2_b[l])]
    wo = w_o[l].astype(BF16)
    ffn_w = (w_up[l][:, :d_ff].astype(BF16), w_up[l][:, d_ff:].astype(BF16), b_up[l][None, :d_ff],
             b_up[l][None, d_ff:], conv_w[l][:, :d_ff], conv_w[l][:, d_ff:], conv_b[l][None, :d_ff],
             conv_b[l][None, d_ff:], w_down[l].astype(BF16), vecs[6], vecs[7])

    xp = jnp.concatenate([jnp.broadcast_to(meta_tokens[None], (B, N_META, D)), x_prompt], axis=1).reshape(B * L, D)
    tm = _row_tile(L, 344)
    cos_p, sin_p = _rope_tables(jnp.arange(L))
    gq, gk, la, gr, ma, mb, kf, vf, gv, qb, kb, vb = _inproj(xp, ln_in_g, ln_in_b, cos_p, sin_p, w_main, w_ga, w_gu,
                                                            b_gate[l], tm)
    b3 = lambda a: a.reshape(B, L, a.shape[-1])
    o_gla, s_p = _gla(b3(gq), b3(gk), b3(la), b3(gv), jnp.zeros((B, GLA_HEADS, GLA_DK, GLA_DV), F32))
    o_diff = _diff_prompt(lam, b3(qb), b3(kb), b3(vb))
    h1 = _merge(xp, o_gla.reshape(B * L, D), gr, o_diff.reshape(B * L, D), ma, mb, vecs, wo, tm)
    y_full, utg, utu = _ffn(h1, *ffn_w, tm=tm, seq_len=L)
    y_prompt = y_full.reshape(B, L, D)[:, N_META:]
    tiles_per_seq = L // tm
    last = lambda u: u.reshape(B, tiles_per_seq, SUBLANES, d_ff)[:, -1, SUBLANES - (CONV_WIDTH - 1):, :]
    conv_p = jnp.concatenate([last(utg), last(utu)], axis=-1)
    k_prompt = kf.reshape(1, B, L, 2 * DIFF_HEADS, HEAD_DIM)
    v_prompt = vf.reshape(1, B, L, DIFF_HEADS, 2 * HEAD_DIM)

    Rs = Bd * T
    xs = x_sample.reshape(Rs, D)
    cos_s, sin_s = _rope_tables(past + jnp.arange(T))
    cos_s, sin_s = jnp.tile(cos_s, (Bd, 1)), jnp.tile(sin_s, (Bd, 1))
    sq, sk, sa, sr, na, nb, ekf, evf, sv, eqb, ekb, evb = _inproj(xs, ln_in_g, ln_in_b, cos_s, sin_s, w_main, w_ga,
                                                                  w_gu, b_gate[l], Rs)
    Tp = GLA_CHUNK
    padT = lambda a: jnp.pad(a.reshape(Bd, T, a.shape[-1]), ((0, 0), (Tp - T, 0), (0, 0)))
    o_gla_s, s_s = _gla(padT(sq), padT(sk), padT(sa), padT(sv), state_gla[l])
    o_gla_s = o_gla_s[:, Tp - T:].reshape(Rs, D)
    q4 = eqb.reshape(Bd, T, 2 * DIFF_HEADS, HEAD_DIM)
    eye = jnp.eye(2 * DIFF_HEADS, dtype=BF16)
    q_rows = (q4.transpose(0, 2, 1, 3)[:, :, :, None, :] * eye[None, :, None, :, None]).reshape(
        Bd, 2 * DIFF_HEADS * T, D)
    Tk = 2 * SUBLANES
    padK = lambda a: jnp.pad(a.reshape(Bd, T, D), ((0, 0), (0, Tk - T), (0, 0)))
    cache_kt = cache_k[l].transpose(0, 2, 3, 1).reshape(n_phys, D, page_size)
    cache_vr = cache_v[l].reshape(n_phys, page_size * DIFF_HEADS, 2 * HEAD_DIM)
    o_diff_s = _diff_sample(page_table, lam, q_rows, padK(ekb), padK(evb), cache_kt, cache_vr, T)
    st = state_ffn_conv[l]
    zero = jnp.zeros((Bd, 1, 2 * d_ff), F32)
    prev1 = jnp.concatenate([st[:, 1:2]] + [zero] * (T - 1), axis=1).reshape(Rs, 2 * d_ff)
    prev2 = jnp.concatenate([st[:, 0:1], st[:, 1:2]] + [zero] * (T - 2), axis=1).reshape(Rs, 2 * d_ff)
    state = (prev1[:, :d_ff], prev1[:, d_ff:], prev2[:, :d_ff], prev2[:, d_ff:])
    h1_s = _merge(xs, o_gla_s, sr, o_diff_s.reshape(Rs, D), na, nb, vecs, wo, Rs)
    y_s, usg, usu = _ffn(h1_s, *ffn_w, tm=Rs, seq_len=T, state=state)
    y_sample = y_s.reshape(Bd, T, D)
    new_rows = lambda u: u.reshape(Bd, T, d_ff)[:, T - (CONV_WIDTH - 1):]
    conv_s = jnp.concatenate([new_rows(usg), new_rows(usu)], axis=-1)
    k_sample = ekf.reshape(1, Bd, T, 2 * DIFF_HEADS, HEAD_DIM)
    v_sample = evf.reshape(1, Bd, T, DIFF_HEADS, 2 * HEAD_DIM)

    return (y_prompt, y_sample, k_prompt, v_prompt, s_p[None], conv_p[None], k_sample, v_sample, s_s[None],
            conv_s[None])
```

```python
import functools
import math

import jax
import jax.numpy as jnp
from jax import lax
from jax.experimental import pallas as pl
from jax.experimental.pallas import tpu as pltpu

F32 = jnp.float32
BF16 = jnp.bfloat16

N_META = 16
GLA_HEADS = 4
GLA_DK = 128
GLA_DV = 256
GLA_GATE_RANK = 16
GLA_GATE_TAU = 16.0
HEAD_DIM = 64
DIFF_HEADS = 8
ROPE_THETA = 10000.0
NORM_EPS = 1e-5
CONV_WIDTH = 3
DEPTH = 1
DEEPNORM_ALPHA = (2.0 * DEPTH) ** 0.25
LAM_INIT = 0.8 - 0.6 * math.exp(-0.3 * 0)
LOG2E = math.log2(math.e)

LANES = 128
SUBLANES = 8
VMEM_LIMIT = 56 * 1024 * 1024
GLA_CHUNK = 128
GLA_DIAG = 8
GLA_HEADS_PER_STEP = 2
ATT_BLOCK = 128
ATT_GROUP = 4
PAGES_PER_STEP = 8


def _const_spec(shape):
    return pl.BlockSpec(shape, lambda *_: (0,) * len(shape), pipeline_mode=pl.Buffered(1))


def _layer_norm(x, g, b):
    mu = jnp.mean(x, axis=-1, keepdims=True)
    xc = x - mu
    var = jnp.mean(xc * xc, axis=-1, keepdims=True)
    return xc * lax.rsqrt(var + NORM_EPS) * g + b


def _sigmoid(x):
    return 1.0 / (1.0 + jnp.exp(-x))


def _dot(a, b):
    return jnp.dot(a, b, preferred_element_type=F32)


def _dot_nt(a, b):
    return lax.dot_general(a, b, (((1,), (1,)), ((), ())), preferred_element_type=F32)


def _inproj_kernel(x_ref, g_ref, b_ref, cos_ref, sin_ref, w_ref, wga_ref, wgu_ref, bg_ref,
                   gq_o, gk_o, la_o, gr_o, ma_o, mb_o, kf_o, vf_o, gv_o, q_o, k_o, v_o):
    h = _layer_norm(x_ref[...], g_ref[...], b_ref[...])
    hb = h.astype(BF16)

    def seg(c0, n):
        return _dot(hb, w_ref[:, c0:c0 + n])

    gq_o[...] = seg(0, 512) * (GLA_DK ** -0.5)
    gk_o[...] = seg(512, 512)
    gv_o[...] = seg(1024, 1024).astype(BF16)
    gr_o[...] = seg(2048, 1024)

    ga = _dot(hb, wga_ref[...])
    xg = _dot(ga.astype(BF16), wgu_ref[...]) + bg_ref[...]
    la_o[...] = (jnp.minimum(xg, 0.0) - jnp.log1p(jnp.exp(-jnp.abs(xg)))) * (1.0 / GLA_GATE_TAU)

    cos = cos_ref[...]
    sin = sin_ref[...]
    lane = lax.broadcasted_iota(jnp.int32, cos.shape, 1)
    first_half = (lane % HEAD_DIM) < (HEAD_DIM // 2)

    def rope(z):
        outs = []
        for g in range(z.shape[1] // LANES):
            zg = z[:, g * LANES:(g + 1) * LANES]
            rot = jnp.where(first_half, pltpu.roll(zg, LANES - HEAD_DIM // 2, 1), pltpu.roll(zg, HEAD_DIM // 2, 1))
            outs.append(zg * cos + rot * sin)
        return jnp.concatenate(outs, axis=1)

    q_o[...] = (rope(seg(3072, 1024)) * (HEAD_DIM ** -0.5 * LOG2E)).astype(BF16)
    kr = rope(seg(4096, 1024))
    kf_o[...] = kr
    k_o[...] = kr.astype(BF16)
    vv = seg(5120, 1024)
    vf_o[...] = vv
    v_o[...] = vv.astype(BF16)
    ma_o[...] = seg(6144, 1024)
    mb_o[...] = seg(7168, 1024)


def _inproj(x, ln_g, ln_b, cos_t, sin_t, w_main, w_ga, w_gu, b_gate, tm):
    R, D = x.shape
    period = cos_t.shape[0] // tm
    row = lambda w: pl.BlockSpec((tm, w), lambda i: (i, 0))
    tab = pl.BlockSpec((tm, LANES), lambda i: (i % period, 0))
    f32_out = lambda w: jax.ShapeDtypeStruct((R, w), F32)
    bf_out = lambda w: jax.ShapeDtypeStruct((R, w), BF16)
    return pl.pallas_call(
        _inproj_kernel,
        grid=(R // tm,),
        in_specs=[row(D), _const_spec((1, D)), _const_spec((1, D)), tab, tab,
                  _const_spec(w_main.shape), _const_spec(w_ga.shape), _const_spec(w_gu.shape),
                  _const_spec((1, 512))],
        out_specs=[row(512), row(512), row(512), row(1024), row(1024), row(1024), row(1024), row(1024),
                   row(1024), row(1024), row(1024), row(1024)],
        out_shape=[f32_out(512), f32_out(512), f32_out(512), f32_out(1024), f32_out(1024), f32_out(1024),
                   f32_out(1024), f32_out(1024), bf_out(1024), bf_out(1024), bf_out(1024), bf_out(1024)],
        compiler_params=pltpu.CompilerParams(dimension_semantics=("parallel",), vmem_limit_bytes=VMEM_LIMIT),
        name="inproj",
    )(x, ln_g.reshape(1, D), ln_b.reshape(1, D), cos_t, sin_t, w_main, w_ga, w_gu, b_gate.reshape(1, 512))


def _split3(x):
    hi = x.astype(BF16)
    r1 = x - hi.astype(F32)
    mid = r1.astype(BF16)
    lo = (r1 - mid.astype(F32)).astype(BF16)
    return hi, mid, lo


def _gla_consts(C):
    row = lax.broadcasted_iota(jnp.int32, (C, C), 0)
    col = lax.broadcasted_iota(jnp.int32, (C, C), 1)
    rowk = lax.broadcasted_iota(jnp.int32, (C, GLA_DK), 0)
    tri = jnp.where(row >= col, 1.0, 0.0).astype(BF16)
    levels = []
    hs = C // 2
    while hs >= GLA_DIAG:
        grp = 2 * hs
        upper = (rowk % grp) >= hs
        keep = ((row // grp) == (col // grp)) & ((row % grp) >= hs) & ((col % grp) < hs)
        levels.append((hs, jnp.where(upper, 0.0, -jnp.inf), jnp.where(upper, -jnp.inf, 0.0),
                       jnp.where(keep, 1.0, 0.0)))
        hs //= 2
    d = GLA_DIAG
    diag = [(jnp.where((rowk % d) >= jj, 0.0, -jnp.inf), jnp.where(col == (row // d) * d + jj, 1.0, 0.0))
            for jj in range(d)]
    return tri, levels, diag


def _gla_intra(q, k, bc, C, levels, diag):
    att = jnp.zeros((C, C), F32)
    for hs, q_bias, k_bias, keep in levels:
        grp = 2 * hs
        refs = jnp.concatenate(
            [jnp.broadcast_to(bc[g * grp + hs - 1:g * grp + hs, :], (grp, GLA_DK)) for g in range(C // grp)], axis=0)
        qs = (q * jnp.exp2((bc - refs) + q_bias)).astype(BF16)
        ks = (k * jnp.exp2((refs - bc) + k_bias)).astype(BF16)
        att = att + _dot_nt(qs, ks) * keep
    d = GLA_DIAG
    nb = C // d
    k3 = k.reshape(nb, d, GLA_DK)
    b3 = bc.reshape(nb, d, GLA_DK)
    for jj, (bias, place) in enumerate(diag):
        kb = jnp.broadcast_to(k3[:, jj:jj + 1, :], (nb, d, GLA_DK)).reshape(C, GLA_DK)
        bb = jnp.broadcast_to(b3[:, jj:jj + 1, :], (nb, d, GLA_DK)).reshape(C, GLA_DK)
        colsum = jnp.sum(q * kb * jnp.exp2((bc - bb) + bias), axis=1, keepdims=True)
        att = att + colsum * place
    return att


def _gla_chunk(q, k, la, v, S, C, consts):
    tri, levels, diag = consts
    hi, mid, lo = _split3(la * LOG2E)
    bc = _dot(tri, hi) + _dot(tri, mid) + _dot(tri, lo)
    o = _dot((q * jnp.exp2(bc)).astype(BF16), S.astype(BF16))
    att = _gla_intra(q, k, bc, C, levels, diag)
    o = o + _dot(att.astype(BF16), v)
    bct = bc.T
    blt = bct[:, C - 1:C]
    kdt = (k.T * jnp.exp2(blt - bct)).astype(BF16)
    S_new = jnp.exp2(blt) * S + _dot(kdt, v)
    return o, S_new


def _gla_kernel(q_ref, k_ref, la_ref, v_ref, s0_ref, o_ref, s_ref, s_scr, *, n_full, tail):
    C = GLA_CHUNK
    consts = _gla_consts(C)
    s_scr[...] = s0_ref[0]

    def do_chunk(r0, skip):
        sl = pl.ds(r0, C)
        for h in range(GLA_HEADS_PER_STEP):
            kc = slice(h * GLA_DK, (h + 1) * GLA_DK)
            vc = slice(h * GLA_DV, (h + 1) * GLA_DV)
            q, k, la, v = q_ref[0, sl, kc], k_ref[0, sl, kc], la_ref[0, sl, kc], v_ref[0, sl, vc]
            if skip:
                live = lax.broadcasted_iota(jnp.int32, (C, GLA_DK), 0) >= skip
                live_v = lax.broadcasted_iota(jnp.int32, (C, GLA_DV), 0) >= skip
                q, k, la = jnp.where(live, q, 0.0), jnp.where(live, k, 0.0), jnp.where(live, la, 0.0)
                v = jnp.where(live_v, v, jnp.zeros_like(v))
            o, S_new = _gla_chunk(q, k, la, v, s_scr[h], C, consts)
            if skip:
                o_ref[0, pl.ds(r0 + skip, C - skip), vc] = o[skip:]
            else:
                o_ref[0, sl, vc] = o
            s_scr[h] = S_new

    def body(i, carry):
        do_chunk(pl.multiple_of(i * C, C), 0)
        return carry

    lax.fori_loop(0, n_full, body, 0, unroll=2)
    if tail:
        do_chunk((n_full - 1) * C + tail, C - tail)
    s_ref[0] = s_scr[...]


def _gla(q, k, la, v, s0):
    B, L, _ = q.shape
    n_full, tail = divmod(L, GLA_CHUNK)
    HP = GLA_HEADS_PER_STEP
    qk = pl.BlockSpec((1, L, HP * GLA_DK), lambda b, h: (b, 0, h))
    vv = pl.BlockSpec((1, L, HP * GLA_DV), lambda b, h: (b, 0, h))
    st = pl.BlockSpec((1, HP, GLA_DK, GLA_DV), lambda b, h: (b, h, 0, 0))
    return pl.pallas_call(
        functools.partial(_gla_kernel, n_full=n_full, tail=tail),
        grid=(B, GLA_HEADS // HP),
        in_specs=[qk, qk, qk, vv, st],
        out_specs=[vv, st],
        out_shape=[jax.ShapeDtypeStruct((B, L, GLA_HEADS * GLA_DV), F32),
                   jax.ShapeDtypeStruct((B, GLA_HEADS, GLA_DK, GLA_DV), F32)],
        scratch_shapes=[pltpu.VMEM((HP, GLA_DK, GLA_DV), F32)],
        compiler_params=pltpu.CompilerParams(dimension_semantics=("parallel", "parallel"),
                                             vmem_limit_bytes=VMEM_LIMIT),
        name="gla",
    )(q, k, la, v, s0)


def _stack_maps(q):
    lane = lax.broadcasted_iota(jnp.int32, q.shape, 1)
    zero = jnp.zeros_like(q)
    return jnp.concatenate([jnp.where(lane < HEAD_DIM, q, zero), jnp.where(lane >= HEAD_DIM, q, zero)], axis=0)


def _diff_prompt_kernel(lam_ref, q_ref, k_ref, v_ref, o_ref, k_scr, v_scr, *, L):
    lam = lam_ref[0]
    C = ATT_BLOCK
    Lpad = k_scr.shape[0]
    n_blocks = Lpad // C
    k_scr[0:L, :] = k_ref[0]
    v_scr[0:L, 0:LANES] = v_ref[0]
    if Lpad > L:
        k_scr[L:Lpad, :] = jnp.zeros((Lpad - L, LANES), BF16)
        v_scr[L:Lpad, 0:LANES] = jnp.zeros((Lpad - L, LANES), BF16)
    v_scr[:, LANES:2 * LANES] = jnp.ones((Lpad, LANES), BF16)

    def q_block(r0, Lk, band):
        q2 = _stack_maps(q_ref[0, pl.ds(r0, C), :])
        s = _dot_nt(q2, k_scr[0:Lk, :])
        sb = s[:, Lk - band:]
        row = r0 + lax.broadcasted_iota(jnp.int32, sb.shape, 0) % C
        col = (Lk - band) + lax.broadcasted_iota(jnp.int32, sb.shape, 1)
        sb = jnp.where(col <= row, sb, -jnp.inf)
        m = jnp.max(sb, axis=1, keepdims=True)
        if Lk > band:
            sa = s[:, :Lk - band]
            m = jnp.maximum(m, jnp.max(sa, axis=1, keepdims=True))
            p = jnp.concatenate([jnp.exp2(sa - m).astype(BF16), jnp.exp2(sb - m).astype(BF16)], axis=1)
        else:
            p = jnp.exp2(sb - m).astype(BF16)
        ov = _dot(p, v_scr[0:Lk, :])
        o = ov[:, :LANES] / ov[:, LANES:]
        o_ref[0, pl.ds(r0, C), :] = o[:C] - lam * o[C:]

    for i0 in range(0, n_blocks, ATT_GROUP):
        i1 = min(i0 + ATT_GROUP, n_blocks)
        first_row = (min(i0 * C, L - C) // C) * C
        for i in range(i0, i1):
            q_block(min(i * C, L - C), i1 * C, i1 * C - first_row)


def _diff_prompt(lam, q, k, v):
    B, L, W = q.shape
    Lpad = -(-L // ATT_BLOCK) * ATT_BLOCK
    blk = pl.BlockSpec((1, L, LANES), lambda b, h, lam_ref: (b, 0, h))
    return pl.pallas_call(
        functools.partial(_diff_prompt_kernel, L=L),
        grid_spec=pltpu.PrefetchScalarGridSpec(
            num_scalar_prefetch=1, grid=(B, DIFF_HEADS), in_specs=[blk, blk, blk], out_specs=blk,
            scratch_shapes=[pltpu.VMEM((Lpad, LANES), BF16), pltpu.VMEM((Lpad, 2 * LANES), BF16)]),
        out_shape=jax.ShapeDtypeStruct((B, L, W), F32),
        compiler_params=pltpu.CompilerParams(dimension_semantics=("parallel", "parallel"),
                                             vmem_limit_bytes=VMEM_LIMIT),
        name="diff_prompt",
    )(lam, q, k, v)


def _diff_sample_kernel(pt_ref, lam_ref, q_ref, kn_ref, vn_ref, *refs, T):
    G = PAGES_PER_STEP
    k_refs, v_refs = refs[:G], refs[G:2 * G]
    o_ref, m_scr, l_scr, acc_scr = refs[2 * G:]
    p_id = pl.program_id(1)
    q = q_ref[0]
    rows_per_head = 2 * T
    page = k_refs[0].shape[2]

    @pl.when(p_id == 0)
    def _():
        s = _dot_nt(q, kn_ref[0])
        t_row = lax.broadcasted_iota(jnp.int32, s.shape, 0) % T
        t_col = lax.broadcasted_iota(jnp.int32, s.shape, 1)
        s = jnp.where(t_col <= t_row, s, -jnp.inf)
        m = jnp.max(s, axis=1, keepdims=True)
        p = jnp.exp2(s - m)
        m_scr[...] = m
        l_scr[...] = jnp.sum(p, axis=1, keepdims=True)
        vn = vn_ref[0]
        acc_scr[...] = jnp.concatenate(
            [_dot(p[h * rows_per_head:(h + 1) * rows_per_head].astype(BF16), vn[:, h * LANES:(h + 1) * LANES])
             for h in range(DIFF_HEADS)], axis=0)

    s = jnp.concatenate([_dot(q, k_refs[g][0].astype(BF16)) for g in range(G)], axis=1)
    m_old = m_scr[...]
    m_new = jnp.maximum(m_old, jnp.max(s, axis=1, keepdims=True))
    p = jnp.exp2(s - m_new)
    corr = jnp.exp2(m_old - m_new)
    pv = []
    for h in range(DIFF_HEADS):
        ph = p[h * rows_per_head:(h + 1) * rows_per_head].astype(BF16)
        vh = jnp.concatenate([v_refs[g][0, pl.ds(h, page, stride=DIFF_HEADS), :].astype(BF16) for g in range(G)],
                             axis=0)
        pv.append(_dot(ph, vh))
    m_scr[...] = m_new
    l_scr[...] = l_scr[...] * corr + jnp.sum(p, axis=1, keepdims=True)
    acc_scr[...] = acc_scr[...] * corr + jnp.concatenate(pv, axis=0)

    @pl.when(p_id == pl.num_programs(1) - 1)
    def _():
        lam = lam_ref[0]
        o = acc_scr[...] / l_scr[...]
        pieces = []
        for h in range(DIFF_HEADS):
            tile = o[h * rows_per_head:(h + 1) * rows_per_head]
            pieces.append(tile[:T] - lam * tile[T:])
        o_ref[0] = jnp.concatenate(pieces, axis=1)


def _diff_sample(page_table, lam, q_rows, k_new, v_new, cache_kt, cache_v, T):
    Bd, R, W = q_rows.shape
    n_pages = page_table.shape[1]
    G = PAGES_PER_STEP
    page = cache_kt.shape[2]
    Tp = k_new.shape[1]
    per_b = lambda shape: pl.BlockSpec((1,) + shape, lambda b, p, pt, lam_ref: (b, 0, 0))
    kpage = lambda g: pl.BlockSpec((1, W, page), lambda b, p, pt, lam_ref, g=g: (pt[b, p * G + g], 0, 0))
    vpage = lambda g: pl.BlockSpec((1, page * DIFF_HEADS, LANES),
                                   lambda b, p, pt, lam_ref, g=g: (pt[b, p * G + g], 0, 0))
    return pl.pallas_call(
        functools.partial(_diff_sample_kernel, T=T),
        grid_spec=pltpu.PrefetchScalarGridSpec(
            num_scalar_prefetch=2, grid=(Bd, n_pages // G),
            in_specs=[per_b((R, W)), per_b((Tp, W)), per_b((Tp, W))]
                     + [kpage(g) for g in range(G)] + [vpage(g) for g in range(G)],
            out_specs=per_b((T, W)),
            scratch_shapes=[pltpu.VMEM((R, 1), F32), pltpu.VMEM((R, 1), F32), pltpu.VMEM((R, LANES), F32)]),
        out_shape=jax.ShapeDtypeStruct((Bd, T, W), F32),
        compiler_params=pltpu.CompilerParams(dimension_semantics=("parallel", "arbitrary"),
                                             vmem_limit_bytes=VMEM_LIMIT),
        name="diff_sample",
    )(page_table, lam, q_rows, k_new, v_new, *([cache_kt] * G), *([cache_v] * G))


def _head_rms(x, g, width):
    outs = []
    for h in range(x.shape[1] // width):
        xh = x[:, h * width:(h + 1) * width]
        outs.append(xh * lax.rsqrt(jnp.mean(xh * xh, axis=1, keepdims=True) + NORM_EPS) * g)
    return jnp.concatenate(outs, axis=1)


def _merge_kernel(x_ref, og_ref, gr_ref, od_ref, ma_ref, mb_ref, lng_ref, lnb_ref, gg_ref, dg_ref, wo_ref,
                  l1g_ref, l1b_ref, h1_ref):
    h = _layer_norm(x_ref[...], lng_ref[...], lnb_ref[...])
    a = _head_rms(og_ref[...], gg_ref[...], GLA_DV)
    a = a * (gr_ref[...] * _sigmoid(gr_ref[...]))
    d = _head_rms(od_ref[...], dg_ref[...], 2 * HEAD_DIM) * (1.0 - LAM_INIT)
    mixed = _sigmoid(ma_ref[...]) * a + _sigmoid(mb_ref[...]) * d
    h1_ref[...] = _layer_norm(DEEPNORM_ALPHA * h + _dot(mixed.astype(BF16), wo_ref[...]), l1g_ref[...],
                              l1b_ref[...])


def _merge(x, og, gr, od, ma, mb, vecs, wo, tm):
    R, D = x.shape
    row = pl.BlockSpec((tm, D), lambda i: (i, 0))
    return pl.pallas_call(
        _merge_kernel,
        grid=(R // tm,),
        in_specs=[row] * 6 + [_const_spec((1, D)), _const_spec((1, D)), _const_spec((1, GLA_DV)),
                              _const_spec((1, 2 * HEAD_DIM)), _const_spec(wo.shape), _const_spec((1, D)),
                              _const_spec((1, D))],
        out_specs=row,
        out_shape=jax.ShapeDtypeStruct((R, D), F32),
        compiler_params=pltpu.CompilerParams(dimension_semantics=("parallel",), vmem_limit_bytes=VMEM_LIMIT),
        name="merge",
    )(x, og, gr, od, ma, mb, *vecs[:4], wo, *vecs[4:6])


def _ffn_kernel(*refs, seq_len, has_state):
    (h1_ref, wug_ref, wuu_ref, bug_ref, buu_ref, cwg_ref, cwu_ref, cbg_ref, cbu_ref, wd_ref, l2g_ref,
     l2b_ref) = refs[:12]
    if has_state:
        p1g_ref, p1u_ref, p2g_ref, p2u_ref, y_ref, ug_ref, uu_ref, cg_scr, cu_scr = refs[12:]
    else:
        y_ref, ug_ref, uu_ref, cg_scr, cu_scr = refs[12:]
    tm = h1_ref.shape[0]
    W = wug_ref.shape[1]

    @pl.when(pl.program_id(0) == 0)
    def _():
        cg_scr[...] = jnp.zeros_like(cg_scr)
        cu_scr[...] = jnp.zeros_like(cu_scr)

    h1 = h1_ref[...]
    h1b = h1.astype(BF16)
    pos = (pl.program_id(0) * tm + lax.broadcasted_iota(jnp.int32, (tm, W), 0)) % seq_len
    row8 = lax.broadcasted_iota(jnp.int32, (SUBLANES, W), 0)

    def shifted(u, s, prev8, state):
        r = pltpu.roll(u, s, 0)
        head = jnp.where(row8 < s, pltpu.roll(prev8, s, 0), r[:SUBLANES])
        r = jnp.concatenate([head, r[SUBLANES:]], axis=0)
        return jnp.where(pos >= s, r, state)

    def conv(u, cw, cb, prev8, st1, st2):
        return cb + cw[0:1] * shifted(u, 2, prev8, st2) + cw[1:2] * shifted(u, 1, prev8, st1) + cw[2:3] * u

    u_g = _dot(h1b, wug_ref[...]) + bug_ref[...]
    u_u = _dot(h1b, wuu_ref[...]) + buu_ref[...]
    if has_state:
        st = (p1g_ref[...], p2g_ref[...], p1u_ref[...], p2u_ref[...])
        ug_ref[...] = u_g
        uu_ref[...] = u_u
    else:
        st = (0.0, 0.0, 0.0, 0.0)
        ug_ref[0] = u_g[tm - SUBLANES:]
        uu_ref[0] = u_u[tm - SUBLANES:]
    gate = conv(u_g, cwg_ref[...], cbg_ref[...], cg_scr[...], st[0], st[1])
    up = conv(u_u, cwu_ref[...], cbu_ref[...], cu_scr[...], st[2], st[3])
    cg_scr[...] = u_g[tm - SUBLANES:]
    cu_scr[...] = u_u[tm - SUBLANES:]
    gelu = 0.5 * gate * (1.0 + jnp.tanh(math.sqrt(2.0 / math.pi) * (gate + 0.044715 * (gate * gate * gate))))
    f = _dot((gelu * up).astype(BF16), wd_ref[...])
    y_ref[...] = _layer_norm(DEEPNORM_ALPHA * h1 + f, l2g_ref[...], l2b_ref[...])


def _ffn(h1, wug, wuu, bug, buu, cwg, cwu, cbg, cbu, wd, l2g, l2b, tm, seq_len, state=None):
    R, D = h1.shape
    W = wug.shape[1]
    n_tiles = R // tm
    has_state = state is not None
    row = pl.BlockSpec((tm, D), lambda i: (i, 0))
    wide = pl.BlockSpec((tm, W), lambda i: (i, 0))
    in_specs = [row] + [_const_spec(a.shape) for a in (wug, wuu, bug, buu, cwg, cwu, cbg, cbu, wd, l2g, l2b)]
    args = [h1, wug, wuu, bug, buu, cwg, cwu, cbg, cbu, wd, l2g, l2b]
    if has_state:
        in_specs += [wide] * 4
        args += list(state)
        u_spec, u_shape = wide, jax.ShapeDtypeStruct((R, W), F32)
    else:
        u_spec = pl.BlockSpec((1, SUBLANES, W), lambda i: (i, 0, 0))
        u_shape = jax.ShapeDtypeStruct((n_tiles, SUBLANES, W), F32)
    return pl.pallas_call(
        functools.partial(_ffn_kernel, seq_len=seq_len, has_state=has_state),
        grid=(n_tiles,),
        in_specs=in_specs,
        out_specs=[row, u_spec, u_spec],
        out_shape=[jax.ShapeDtypeStruct((R, D), F32), u_shape, u_shape],
        scratch_shapes=[pltpu.VMEM((SUBLANES, W), F32), pltpu.VMEM((SUBLANES, W), F32)],
        compiler_params=pltpu.CompilerParams(dimension_semantics=("arbitrary",), vmem_limit_bytes=VMEM_LIMIT),
        name="ffn_state" if has_state else "ffn",
    )(*args)


def _rope_tables(pos):
    inv_freq = 1.0 / (ROPE_THETA ** (jnp.arange(0, HEAD_DIM, 2, dtype=F32) / HEAD_DIM))
    ang = pos.astype(F32)[:, None] * inv_freq[None, :]
    ang = jnp.concatenate([ang, ang], axis=-1)
    sign = jnp.concatenate([-jnp.ones((HEAD_DIM // 2,), F32), jnp.ones((HEAD_DIM // 2,), F32)])
    return jnp.tile(jnp.cos(ang), (1, 2)), jnp.tile(jnp.sin(ang) * sign, (1, 2))


def _row_tile(L, target):
    best = SUBLANES
    for t in range(SUBLANES, target + 1, SUBLANES):
        if L % t == 0:
            best = t
    return best


def kernel(x_prompt, x_sample, cache_k, cache_v, state_gla, state_ffn_conv, page_table, meta_tokens, ln_in_g, ln_in_b, w_in, w_gate_up, b_gate, gla_norm_g, lambda_q1, lambda_k1, lambda_q2, lambda_k2, diff_norm_g, w_o, ln1_g, ln1_b, w_up, b_up, conv_w, conv_b, w_down, ln2_g, ln2_b):
    B, S, D = x_prompt.shape
    Bd, T, _ = x_sample.shape
    L = N_META + S
    n_phys, page_size = cache_k.shape[1], cache_k.shape[2]
    past = page_table.shape[1] * page_size
    d_ff = w_down.shape[1]
    l = 0

    wi = w_in[l]
    offs = [0, 512, 1024, 2048, 3072, 3088, 4112, 5136, 6160, 7184, 8208]
    seg = lambda i: wi[:, offs[i]:offs[i + 1]]
    w_main = jnp.concatenate([seg(0), seg(1), seg(2), seg(3), seg(5), seg(6), seg(7), seg(8), seg(9)],
                             axis=1).astype(BF16)
    w_ga = jnp.pad(seg(4), ((0, 0), (0, LANES - GLA_GATE_RANK))).astype(BF16)
    w_gu = jnp.pad(w_gate_up[l], ((0, LANES - GLA_GATE_RANK), (0, 0))).astype(BF16)
    lam = (jnp.exp(jnp.sum(lambda_q1[l] * lambda_k1[l])) - jnp.exp(jnp.sum(lambda_q2[l] * lambda_k2[l]))
           + LAM_INIT).reshape(1).astype(F32)
    r1 = lambda v: v.reshape(1, -1)
    vecs = [r1(ln_in_g), r1(ln_in_b), r1(gla_norm_g[l]), r1(diff_norm_g[l]), r1(ln1_g[l]), r1(ln1_b[l]),
            r1(ln2_g[l]), r1(ln2_b[l])]
    wo = w_o[l].astype(BF16)
    ffn_w = (w_up[l][:, :d_ff].astype(BF16), w_up[l][:, d_ff:].astype(BF16), b_up[l][None, :d_ff],
             b_up[l][None, d_ff:], conv_w[l][:, :d_ff], conv_w[l][:, d_ff:], conv_b[l][None, :d_ff],
             conv_b[l][None, d_ff:], w_down[l].astype(BF16), vecs[6], vecs[7])

    xp = jnp.concatenate([jnp.broadcast_to(meta_tokens[None], (B, N_META, D)), x_prompt], axis=1).reshape(B * L, D)
    tm = _row_tile(L, 344)
    cos_p, sin_p = _rope_tables(jnp.arange(L))
    gq, gk, la, gr, ma, mb, kf, vf, gv, qb, kb, vb = _inproj(xp, ln_in_g, ln_in_b, cos_p, sin_p, w_main, w_ga, w_gu,
                                                            b_gate[l], tm)
    b3 = lambda a: a.reshape(B, L, a.shape[-1])
    o_gla, s_p = _gla(b3(gq), b3(gk), b3(la), b3(gv), jnp.zeros((B, GLA_HEADS, GLA_DK, GLA_DV), F32))
    o_diff = _diff_prompt(lam, b3(qb), b3(kb), b3(vb))
    h1 = _merge(xp, o_gla.reshape(B * L, D), gr, o_diff.reshape(B * L, D), ma, mb, vecs, wo, tm)
    y_full, utg, utu = _ffn(h1, *ffn_w, tm=tm, seq_len=L)
    y_prompt = y_full.reshape(B, L, D)[:, N_META:]
    tiles_per_seq = L // tm
    last = lambda u: u.reshape(B, tiles_per_seq, SUBLANES, d_ff)[:, -1, SUBLANES - (CONV_WIDTH - 1):, :]
    conv_p = jnp.concatenate([last(utg), last(utu)], axis=-1)
    k_prompt = kf.reshape(1, B, L, 2 * DIFF_HEADS, HEAD_DIM)
    v_prompt = vf.reshape(1, B, L, DIFF_HEADS, 2 * HEAD_DIM)

    Rs = Bd * T
    xs = x_sample.reshape(Rs, D)
    cos_s, sin_s = _rope_tables(past + jnp.arange(T))
    cos_s, sin_s = jnp.tile(cos_s, (Bd, 1)), jnp.tile(sin_s, (Bd, 1))
    sq, sk, sa, sr, na, nb, ekf, evf, sv, eqb, ekb, evb = _inproj(xs, ln_in_g, ln_in_b, cos_s, sin_s, w_main, w_ga,
                                                                  w_gu, b_gate[l], Rs)
    Tp = GLA_CHUNK
    padT = lambda a: jnp.pad(a.reshape(Bd, T, a.shape[-1]), ((0, 0), (Tp - T, 0), (0, 0)))
    o_gla_s, s_s = _gla(padT(sq), padT(sk), padT(sa), padT(sv), state_gla[l])
    o_gla_s = o_gla_s[:, Tp - T:].reshape(Rs, D)
    q4 = eqb.reshape(Bd, T, 2 * DIFF_HEADS, HEAD_DIM)
    eye = jnp.eye(2 * DIFF_HEADS, dtype=BF16)
    q_rows = (q4.transpose(0, 2, 1, 3)[:, :, :, None, :] * eye[None, :, None, :, None]).reshape(
        Bd, 2 * DIFF_HEADS * T, D)
    Tk = 2 * SUBLANES
    padK = lambda a: jnp.pad(a.reshape(Bd, T, D), ((0, 0), (0, Tk - T), (0, 0)))
    cache_kt = cache_k[l].transpose(0, 2, 3, 1).reshape(n_phys, D, page_size)
    cache_vr = cache_v[l].reshape(n_phys, page_size * DIFF_HEADS, 2 * HEAD_DIM)
    o_diff_s = _diff_sample(page_table, lam, q_rows, padK(ekb), padK(evb), cache_kt, cache_vr, T)
    st = state_ffn_conv[l]
    zero = jnp.zeros((Bd, 1, 2 * d_ff), F32)
    prev1 = jnp.concatenate([st[:, 1:2]] + [zero] * (T - 1), axis=1).reshape(Rs, 2 * d_ff)
    prev2 = jnp.concatenate([st[:, 0:1], st[:, 1:2]] + [zero] * (T - 2), axis=1).reshape(Rs, 2 * d_ff)
    state = (prev1[:, :d_ff], prev1[:, d_ff:], prev2[:, :d_ff], prev2[:, d_ff:])
    h1_s = _merge(xs, o_gla_s, sr, o_diff_s.reshape(Rs, D), na, nb, vecs, wo, Rs)
    y_s, usg, usu = _ffn(h1_s, *ffn_w, tm=Rs, seq_len=T, state=state)
    y_sample = y_s.reshape(Bd, T, D)
    new_rows = lambda u: u.reshape(Bd, T, d_ff)[:, T - (CONV_WIDTH - 1):]
    conv_s = jnp.concatenate([new_rows(usg), new_rows(usu)], axis=-1)
    k_sample = ekf.reshape(1, Bd, T, 2 * DIFF_HEADS, HEAD_DIM)
    v_sample = evf.reshape(1, Bd, T, DIFF_HEADS, 2 * HEAD_DIM)

    return (y_prompt, y_sample, k_prompt, v_prompt, s_p[None], conv_p[None], k_sample, v_sample, s_s[None],
            conv_s[None])
```

```python
import functools
import math

import jax
import jax.numpy as jnp
from jax import lax
from jax.experimental import pallas as pl
from jax.experimental.pallas import tpu as pltpu

F32 = jnp.float32
BF16 = jnp.bfloat16

N_META = 16
GLA_HEADS = 4
GLA_DK = 128
GLA_DV = 256
GLA_GATE_RANK = 16
GLA_GATE_TAU = 16.0
HEAD_DIM = 64
DIFF_HEADS = 8
ROPE_THETA = 10000.0
NORM_EPS = 1e-5
CONV_WIDTH = 3
DEPTH = 1
DEEPNORM_ALPHA = (2.0 * DEPTH) ** 0.25
LAM_INIT = 0.8 - 0.6 * math.exp(-0.3 * 0)
LOG2E = math.log2(math.e)

LANES = 128
SUBLANES = 8
VMEM_LIMIT = 56 * 1024 * 1024
GLA_CHUNK = 128
GLA_DIAG = 8
GLA_HEADS_PER_STEP = 2
ATT_BLOCK = 128
ATT_GROUP = 4
PAGES_PER_STEP = 8


def _const_spec(shape):
    return pl.BlockSpec(shape, lambda *_: (0,) * len(shape), pipeline_mode=pl.Buffered(1))


def _layer_norm(x, g, b):
    mu = jnp.mean(x, axis=-1, keepdims=True)
    xc = x - mu
    var = jnp.mean(xc * xc, axis=-1, keepdims=True)
    return xc * lax.rsqrt(var + NORM_EPS) * g + b


def _sigmoid(x):
    return 1.0 / (1.0 + jnp.exp(-x))


def _dot(a, b):
    return jnp.dot(a, b, preferred_element_type=F32)


def _dot_nt(a, b):
    return lax.dot_general(a, b, (((1,), (1,)), ((), ())), preferred_element_type=F32)


def _inproj_kernel(x_ref, g_ref, b_ref, cos_ref, sin_ref, w_ref, wga_ref, wgu_ref, bg_ref,
                   gq_o, gk_o, la_o, gr_o, ma_o, mb_o, kf_o, vf_o, gv_o, q_o, k_o, v_o):
    h = _layer_norm(x_ref[...], g_ref[...], b_ref[...])
    hb = h.astype(BF16)

    def seg(c0, n):
        return _dot(hb, w_ref[:, c0:c0 + n])

    gq_o[...] = seg(0, 512) * (GLA_DK ** -0.5)
    gk_o[...] = seg(512, 512)
    gv_o[...] = seg(1024, 1024).astype(BF16)
    gr_o[...] = seg(2048, 1024)

    ga = _dot(hb, wga_ref[...])
    xg = _dot(ga.astype(BF16), wgu_ref[...]) + bg_ref[...]
    la_o[...] = (jnp.minimum(xg, 0.0) - jnp.log1p(jnp.exp(-jnp.abs(xg)))) * (1.0 / GLA_GATE_TAU)

    cos = cos_ref[...]
    sin = sin_ref[...]
    lane = lax.broadcasted_iota(jnp.int32, cos.shape, 1)
    first_half = (lane % HEAD_DIM) < (HEAD_DIM // 2)

    def rope(z):
        outs = []
        for g in range(z.shape[1] // LANES):
            zg = z[:, g * LANES:(g + 1) * LANES]
            rot = jnp.where(first_half, pltpu.roll(zg, LANES - HEAD_DIM // 2, 1), pltpu.roll(zg, HEAD_DIM // 2, 1))
            outs.append(zg * cos + rot * sin)
        return jnp.concatenate(outs, axis=1)

    q_o[...] = (rope(seg(3072, 1024)) * (HEAD_DIM ** -0.5 * LOG2E)).astype(BF16)
    kr = rope(seg(4096, 1024))
    kf_o[...] = kr
    k_o[...] = kr.astype(BF16)
    vv = seg(5120, 1024)
    vf_o[...] = vv
    v_o[...] = vv.astype(BF16)
    ma_o[...] = seg(6144, 1024)
    mb_o[...] = seg(7168, 1024)


def _inproj(x, ln_g, ln_b, cos_t, sin_t, w_main, w_ga, w_gu, b_gate, tm):
    R, D = x.shape
    period = cos_t.shape[0] // tm
    row = lambda w: pl.BlockSpec((tm, w), lambda i: (i, 0))
    tab = pl.BlockSpec((tm, LANES), lambda i: (i % period, 0))
    f32_out = lambda w: jax.ShapeDtypeStruct((R, w), F32)
    bf_out = lambda w: jax.ShapeDtypeStruct((R, w), BF16)
    return pl.pallas_call(
        _inproj_kernel,
        grid=(R // tm,),
        in_specs=[row(D), _const_spec((1, D)), _const_spec((1, D)), tab, tab,
                  _const_spec(w_main.shape), _const_spec(w_ga.shape), _const_spec(w_gu.shape),
                  _const_spec((1, 512))],
        out_specs=[row(512), row(512), row(512), row(1024), row(1024), row(1024), row(1024), row(1024),
                   row(1024), row(1024), row(1024), row(1024)],
        out_shape=[f32_out(512), f32_out(512), f32_out(512), f32_out(1024), f32_out(1024), f32_out(1024),
                   f32_out(1024), f32_out(1024), bf_out(1024), bf_out(1024), bf_out(1024), bf_out(1024)],
        compiler_params=pltpu.CompilerParams(dimension_semantics=("parallel",), vmem_limit_bytes=VMEM_LIMIT),
        name="inproj",
    )(x, ln_g.reshape(1, D), ln_b.reshape(1, D), cos_t, sin_t, w_main, w_ga, w_gu, b_gate.reshape(1, 512))


def _split3(x):
    hi = x.astype(BF16)
    r1 = x - hi.astype(F32)
    mid = r1.astype(BF16)
    lo = (r1 - mid.astype(F32)).astype(BF16)
    return hi, mid, lo


def _gla_consts(C):
    row = lax.broadcasted_iota(jnp.int32, (C, C), 0)
    col = lax.broadcasted_iota(jnp.int32, (C, C), 1)
    rowk = lax.broadcasted_iota(jnp.int32, (C, GLA_DK), 0)
    tri = jnp.where(row >= col, 1.0, 0.0).astype(BF16)
    levels = []
    hs = C // 2
    while hs >= GLA_DIAG:
        grp = 2 * hs
        upper = (rowk % grp) >= hs
        keep = ((row // grp) == (col // grp)) & ((row % grp) >= hs) & ((col % grp) < hs)
        levels.append((hs, jnp.where(upper, 0.0, -jnp.inf), jnp.where(upper, -jnp.inf, 0.0),
                       jnp.where(keep, 1.0, 0.0)))
        hs //= 2
    d = GLA_DIAG
    diag = [(jnp.where((rowk % d) >= jj, 0.0, -jnp.inf), jnp.where(col == (row // d) * d + jj, 1.0, 0.0))
            for jj in range(d)]
    return tri, levels, diag


def _gla_intra(q, k, bc, C, levels, diag):
    att = jnp.zeros((C, C), F32)
    for hs, q_bias, k_bias, keep in levels:
        grp = 2 * hs
        refs = jnp.concatenate(
            [jnp.broadcast_to(bc[g * grp + hs - 1:g * grp + hs, :], (grp, GLA_DK)) for g in range(C // grp)], axis=0)
        qs = (q * jnp.exp2((bc - refs) + q_bias)).astype(BF16)
        ks = (k * jnp.exp2((refs - bc) + k_bias)).astype(BF16)
        att = att + _dot_nt(qs, ks) * keep
    d = GLA_DIAG
    nb = C // d
    k3 = k.reshape(nb, d, GLA_DK)
    b3 = bc.reshape(nb, d, GLA_DK)
    for jj, (bias, place) in enumerate(diag):
        kb = jnp.broadcast_to(k3[:, jj:jj + 1, :], (nb, d, GLA_DK)).reshape(C, GLA_DK)
        bb = jnp.broadcast_to(b3[:, jj:jj + 1, :], (nb, d, GLA_DK)).reshape(C, GLA_DK)
        colsum = jnp.sum(q * kb * jnp.exp2((bc - bb) + bias), axis=1, keepdims=True)
        att = att + colsum * place
    return att


def _gla_chunk(q, k, la, v, S, C, consts):
    tri, levels, diag = consts
    hi, mid, lo = _split3(la * LOG2E)
    bc = _dot(tri, hi) + _dot(tri, mid) + _dot(tri, lo)
    o = _dot((q * jnp.exp2(bc)).astype(BF16), S.astype(BF16))
    att = _gla_intra(q, k, bc, C, levels, diag)
    o = o + _dot(att.astype(BF16), v)
    bct = bc.T
    blt = bct[:, C - 1:C]
    kdt = (k.T * jnp.exp2(blt - bct)).astype(BF16)
    S_new = jnp.exp2(blt) * S + _dot(kdt, v)
    return o, S_new


def _gla_kernel(q_ref, k_ref, la_ref, v_ref, s0_ref, o_ref, s_ref, s_scr, *, n_full, tail):
    C = GLA_CHUNK
    consts = _gla_consts(C)
    s_scr[...] = s0_ref[0]

    def do_chunk(r0, skip):
        sl = pl.ds(r0, C)
        for h in range(GLA_HEADS_PER_STEP):
            kc = slice(h * GLA_DK, (h + 1) * GLA_DK)
            vc = slice(h * GLA_DV, (h + 1) * GLA_DV)
            q, k, la, v = q_ref[0, sl, kc], k_ref[0, sl, kc], la_ref[0, sl, kc], v_ref[0, sl, vc]
            if skip:
                live = lax.broadcasted_iota(jnp.int32, (C, GLA_DK), 0) >= skip
                live_v = lax.broadcasted_iota(jnp.int32, (C, GLA_DV), 0) >= skip
                q, k, la = jnp.where(live, q, 0.0), jnp.where(live, k, 0.0), jnp.where(live, la, 0.0)
                v = jnp.where(live_v, v, jnp.zeros_like(v))
            o, S_new = _gla_chunk(q, k, la, v, s_scr[h], C, consts)
            if skip:
                o_ref[0, pl.ds(r0 + skip, C - skip), vc] = o[skip:]
            else:
                o_ref[0, sl, vc] = o
            s_scr[h] = S_new

    def body(i, carry):
        do_chunk(pl.multiple_of(i * C, C), 0)
        return carry

    lax.fori_loop(0, n_full, body, 0, unroll=2)
    if tail:
        do_chunk((n_full - 1) * C + tail, C - tail)
    s_ref[0] = s_scr[...]


def _gla(q, k, la, v, s0):
    B, L, _ = q.shape
    n_full, tail = divmod(L, GLA_CHUNK)
    HP = GLA_HEADS_PER_STEP
    qk = pl.BlockSpec((1, L, HP * GLA_DK), lambda b, h: (b, 0, h))
    vv = pl.BlockSpec((1, L, HP * GLA_DV), lambda b, h: (b, 0, h))
    st = pl.BlockSpec((1, HP, GLA_DK, GLA_DV), lambda b, h: (b, h, 0, 0))
    return pl.pallas_call(
        functools.partial(_gla_kernel, n_full=n_full, tail=tail),
        grid=(B, GLA_HEADS // HP),
        in_specs=[qk, qk, qk, vv, st],
        out_specs=[vv, st],
        out_shape=[jax.ShapeDtypeStruct((B, L, GLA_HEADS * GLA_DV), F32),
                   jax.ShapeDtypeStruct((B, GLA_HEADS, GLA_DK, GLA_DV), F32)],
        scratch_shapes=[pltpu.VMEM((HP, GLA_DK, GLA_DV), F32)],
        compiler_params=pltpu.CompilerParams(dimension_semantics=("parallel", "parallel"),
                                             vmem_limit_bytes=VMEM_LIMIT),
        name="gla",
    )(q, k, la, v, s0)


def _stack_maps(q):
    lane = lax.broadcasted_iota(jnp.int32, q.shape, 1)
    zero = jnp.zeros_like(q)
    return jnp.concatenate([jnp.where(lane < HEAD_DIM, q, zero), jnp.where(lane >= HEAD_DIM, q, zero)], axis=0)


def _split_blocks(n_blocks, parts):
    total = n_blocks * (n_blocks + 1) // 2
    bounds, acc = [0], 0
    for i in range(n_blocks):
        acc += i + 1
        if len(bounds) < parts and acc * parts >= total * len(bounds) and n_blocks - (i + 1) >= parts - len(bounds):
            bounds.append(i + 1)
    bounds.append(n_blocks)
    return [(bounds[p], bounds[p + 1]) for p in range(parts)]


def _diff_attn_kernel(pt_ref, lam_ref, q_ref, k_ref, v_ref, qs_ref, kn_ref, vn_ref, *refs, L, T, parts,
                      steps_per_seq):
    G = PAGES_PER_STEP
    k_refs, v_refs = refs[:G], refs[G:2 * G]
    o_ref, os_ref, k_scr, v_scr, m_scr, l_scr, acc_scr = refs[2 * G:]
    lam = lam_ref[0]
    C = ATT_BLOCK
    Lpad = k_scr.shape[0]
    j = pl.program_id(2)
    step = (pl.program_id(0) * pl.num_programs(1) + pl.program_id(1)) * pl.num_programs(2) + j
    seq_step = step % steps_per_seq
    qs = qs_ref[0]
    rows_per_head = 2 * T
    page = k_refs[0].shape[2]

    @pl.when(j == 0)
    def _():
        k_scr[0:L, :] = k_ref[0]
        v_scr[0:L, 0:LANES] = v_ref[0]
        if Lpad > L:
            k_scr[L:Lpad, :] = jnp.zeros((Lpad - L, LANES), BF16)
            v_scr[L:Lpad, 0:LANES] = jnp.zeros((Lpad - L, LANES), BF16)
        v_scr[:, LANES:2 * LANES] = jnp.ones((Lpad, LANES), BF16)

    @pl.when(seq_step == 0)
    def _():
        s = _dot_nt(qs, kn_ref[0])
        t_row = lax.broadcasted_iota(jnp.int32, s.shape, 0) % T
        t_col = lax.broadcasted_iota(jnp.int32, s.shape, 1)
        s = jnp.where(t_col <= t_row, s, -jnp.inf)
        m = jnp.max(s, axis=1, keepdims=True)
        p = jnp.exp2(s - m)
        m_scr[...] = m
        l_scr[...] = jnp.sum(p, axis=1, keepdims=True)
        vn = vn_ref[0]
        acc_scr[...] = jnp.concatenate(
            [_dot(p[h * rows_per_head:(h + 1) * rows_per_head].astype(BF16), vn[:, h * LANES:(h + 1) * LANES])
             for h in range(DIFF_HEADS)], axis=0)

    def sample_scores():
        s = jnp.concatenate([_dot(qs, k_refs[g][0].astype(BF16)) for g in range(G)], axis=1)
        m_old = m_scr[...]
        m_new = jnp.maximum(m_old, jnp.max(s, axis=1, keepdims=True))
        return jnp.exp2(s - m_new), jnp.exp2(m_old - m_new), m_new

    def sample_values(p, corr, m_new):
        pv = []
        for h in range(DIFF_HEADS):
            ph = p[h * rows_per_head:(h + 1) * rows_per_head].astype(BF16)
            vh = jnp.concatenate(
                [v_refs[g][0, pl.ds(h, page, stride=DIFF_HEADS), :].astype(BF16) for g in range(G)], axis=0)
            pv.append(_dot(ph, vh))
        m_scr[...] = m_new
        l_scr[...] = l_scr[...] * corr + jnp.sum(p, axis=1, keepdims=True)
        acc_scr[...] = acc_scr[...] * corr + jnp.concatenate(pv, axis=0)

    def q_block(r0, Lk, band):
        q2 = _stack_maps(q_ref[0, pl.ds(r0, C), :])
        s = _dot_nt(q2, k_scr[0:Lk, :])
        sb = s[:, Lk - band:]
        row = r0 + lax.broadcasted_iota(jnp.int32, sb.shape, 0) % C
        col = (Lk - band) + lax.broadcasted_iota(jnp.int32, sb.shape, 1)
        sb = jnp.where(col <= row, sb, -jnp.inf)
        m = jnp.max(sb, axis=1, keepdims=True)
        if Lk > band:
            sa = s[:, :Lk - band]
            m = jnp.maximum(m, jnp.max(sa, axis=1, keepdims=True))
            p = jnp.concatenate([jnp.exp2(sa - m).astype(BF16), jnp.exp2(sb - m).astype(BF16)], axis=1)
        else:
            p = jnp.exp2(sb - m).astype(BF16)
        ov = _dot(p, v_scr[0:Lk, :])
        o = ov[:, :LANES] / ov[:, LANES:]
        o_ref[0, pl.ds(r0, C), :] = o[:C] - lam * o[C:]

    for part, (b0, b1) in enumerate(parts):
        @pl.when(j == part)
        def _(b0=b0, b1=b1):
            page_state = sample_scores()
            for i0 in range(b0, b1, ATT_GROUP):
                i1 = min(i0 + ATT_GROUP, b1)
                first_row = (min(i0 * C, L - C) // C) * C
                for i in range(i0, i1):
                    q_block(min(i * C, L - C), i1 * C, i1 * C - first_row)
            sample_values(*page_state)

    @pl.when(seq_step == steps_per_seq - 1)
    def _():
        o = acc_scr[...] / l_scr[...]
        pieces = []
        for h in range(DIFF_HEADS):
            tile = o[h * rows_per_head:(h + 1) * rows_per_head]
            pieces.append(tile[:T] - lam * tile[T:])
        os_ref[0] = jnp.concatenate(pieces, axis=1)


def _diff_attn(page_table, lam, q, k, v, q_rows, k_new, v_new, cache_kt, cache_v, T):
    B, L, W = q.shape
    Bd, R, _ = q_rows.shape
    n_pages = page_table.shape[1]
    G = PAGES_PER_STEP
    page = cache_kt.shape[2]
    Tp = k_new.shape[1]
    Lpad = -(-L // ATT_BLOCK) * ATT_BLOCK
    steps_per_seq = n_pages // G
    J = (Bd * steps_per_seq) // (B * DIFF_HEADS)
    assert J * B * DIFF_HEADS == Bd * steps_per_seq and n_pages % G == 0 and 1 <= J <= Lpad // ATT_BLOCK
    parts = _split_blocks(Lpad // ATT_BLOCK, J)

    def seq_of(b, h, j):
        return ((b * DIFF_HEADS + h) * J + j) // steps_per_seq

    def page_of(b, h, j, pt, g):
        step = (b * DIFF_HEADS + h) * J + j
        return pt[step // steps_per_seq, (step % steps_per_seq) * G + g]

    blk = pl.BlockSpec((1, L, LANES), lambda b, h, j, pt, lam_ref: (b, 0, h))
    per_seq = lambda shape: pl.BlockSpec((1,) + shape, lambda b, h, j, pt, lam_ref: (seq_of(b, h, j), 0, 0))
    kpage = lambda g: pl.BlockSpec((1, W, page), lambda b, h, j, pt, lam_ref, g=g: (page_of(b, h, j, pt, g), 0, 0))
    vpage = lambda g: pl.BlockSpec((1, page * DIFF_HEADS, LANES),
                                   lambda b, h, j, pt, lam_ref, g=g: (page_of(b, h, j, pt, g), 0, 0))
    return pl.pallas_call(
        functools.partial(_diff_attn_kernel, L=L, T=T, parts=parts, steps_per_seq=steps_per_seq),
        grid_spec=pltpu.PrefetchScalarGridSpec(
            num_scalar_prefetch=2, grid=(B, DIFF_HEADS, J),
            in_specs=[blk, blk, blk, per_seq((R, W)), per_seq((Tp, W)), per_seq((Tp, W))]
                     + [kpage(g) for g in range(G)] + [vpage(g) for g in range(G)],
            out_specs=[blk, per_seq((T, W))],
            scratch_shapes=[pltpu.VMEM((Lpad, LANES), BF16), pltpu.VMEM((Lpad, 2 * LANES), BF16),
                            pltpu.VMEM((R, 1), F32), pltpu.VMEM((R, 1), F32), pltpu.VMEM((R, LANES), F32)]),
        out_shape=[jax.ShapeDtypeStruct((B, L, W), F32), jax.ShapeDtypeStruct((Bd, T, W), F32)],
        compiler_params=pltpu.CompilerParams(dimension_semantics=("arbitrary", "arbitrary", "arbitrary"),
                                             vmem_limit_bytes=VMEM_LIMIT),
        name="diff_attn",
    )(page_table, lam, q, k, v, q_rows, k_new, v_new, *([cache_kt] * G), *([cache_v] * G))


def _head_rms(x, g, width):
    outs = []
    for h in range(x.shape[1] // width):
        xh = x[:, h * width:(h + 1) * width]
        outs.append(xh * lax.rsqrt(jnp.mean(xh * xh, axis=1, keepdims=True) + NORM_EPS) * g)
    return jnp.concatenate(outs, axis=1)


def _merge_kernel(x_ref, og_ref, gr_ref, od_ref, ma_ref, mb_ref, lng_ref, lnb_ref, gg_ref, dg_ref, wo_ref,
                  l1g_ref, l1b_ref, h1_ref):
    h = _layer_norm(x_ref[...], lng_ref[...], lnb_ref[...])
    a = _head_rms(og_ref[...], gg_ref[...], GLA_DV)
    a = a * (gr_ref[...] * _sigmoid(gr_ref[...]))
    d = _head_rms(od_ref[...], dg_ref[...], 2 * HEAD_DIM) * (1.0 - LAM_INIT)
    mixed = _sigmoid(ma_ref[...]) * a + _sigmoid(mb_ref[...]) * d
    h1_ref[...] = _layer_norm(DEEPNORM_ALPHA * h + _dot(mixed.astype(BF16), wo_ref[...]), l1g_ref[...],
                              l1b_ref[...])


def _merge(x, og, gr, od, ma, mb, vecs, wo, tm):
    R, D = x.shape
    row = pl.BlockSpec((tm, D), lambda i: (i, 0))
    return pl.pallas_call(
        _merge_kernel,
        grid=(R // tm,),
        in_specs=[row] * 6 + [_const_spec((1, D)), _const_spec((1, D)), _const_spec((1, GLA_DV)),
                              _const_spec((1, 2 * HEAD_DIM)), _const_spec(wo.shape), _const_spec((1, D)),
                              _const_spec((1, D))],
        out_specs=row,
        out_shape=jax.ShapeDtypeStruct((R, D), F32),
        compiler_params=pltpu.CompilerParams(dimension_semantics=("parallel",), vmem_limit_bytes=VMEM_LIMIT),
        name="merge",
    )(x, og, gr, od, ma, mb, *vecs[:4], wo, *vecs[4:6])


def _ffn_kernel(*refs, seq_len, has_state):
    (h1_ref, wug_ref, wuu_ref, bug_ref, buu_ref, cwg_ref, cwu_ref, cbg_ref, cbu_ref, wd_ref, l2g_ref,
     l2b_ref) = refs[:12]
    if has_state:
        p1g_ref, p1u_ref, p2g_ref, p2u_ref, y_ref, ug_ref, uu_ref, cg_scr, cu_scr = refs[12:]
    else:
        y_ref, ug_ref, uu_ref, cg_scr, cu_scr = refs[12:]
    tm = h1_ref.shape[0]
    W = wug_ref.shape[1]

    @pl.when(pl.program_id(0) == 0)
    def _():
        cg_scr[...] = jnp.zeros_like(cg_scr)
        cu_scr[...] = jnp.zeros_like(cu_scr)

    h1 = h1_ref[...]
    h1b = h1.astype(BF16)
    pos = (pl.program_id(0) * tm + lax.broadcasted_iota(jnp.int32, (tm, W), 0)) % seq_len
    row8 = lax.broadcasted_iota(jnp.int32, (SUBLANES, W), 0)

    def shifted(u, s, prev8, state):
        r = pltpu.roll(u, s, 0)
        head = jnp.where(row8 < s, pltpu.roll(prev8, s, 0), r[:SUBLANES])
        r = jnp.concatenate([head, r[SUBLANES:]], axis=0)
        return jnp.where(pos >= s, r, state)

    def conv(u, cw, cb, prev8, st1, st2):
        return cb + cw[0:1] * shifted(u, 2, prev8, st2) + cw[1:2] * shifted(u, 1, prev8, st1) + cw[2:3] * u

    u_g = _dot(h1b, wug_ref[...]) + bug_ref[...]
    u_u = _dot(h1b, wuu_ref[...]) + buu_ref[...]
    if has_state:
        st = (p1g_ref[...], p2g_ref[...], p1u_ref[...], p2u_ref[...])
        ug_ref[...] = u_g
        uu_ref[...] = u_u
    else:
        st = (0.0, 0.0, 0.0, 0.0)
        ug_ref[0] = u_g[tm - SUBLANES:]
        uu_ref[0] = u_u[tm - SUBLANES:]
    gate = conv(u_g, cwg_ref[...], cbg_ref[...], cg_scr[...], st[0], st[1])
    up = conv(u_u, cwu_ref[...], cbu_ref[...], cu_scr[...], st[2], st[3])
    cg_scr[...] = u_g[tm - SUBLANES:]
    cu_scr[...] = u_u[tm - SUBLANES:]
    gelu = 0.5 * gate * (1.0 + jnp.tanh(math.sqrt(2.0 / math.pi) * (gate + 0.044715 * (gate * gate * gate))))
    f = _dot((gelu * up).astype(BF16), wd_ref[...])
    y_ref[...] = _layer_norm(DEEPNORM_ALPHA * h1 + f, l2g_ref[...], l2b_ref[...])


def _ffn(h1, wug, wuu, bug, buu, cwg, cwu, cbg, cbu, wd, l2g, l2b, tm, seq_len, state=None):
    R, D = h1.shape
    W = wug.shape[1]
    n_tiles = R // tm
    has_state = state is not None
    row = pl.BlockSpec((tm, D), lambda i: (i, 0))
    wide = pl.BlockSpec((tm, W), lambda i: (i, 0))
    in_specs = [row] + [_const_spec(a.shape) for a in (wug, wuu, bug, buu, cwg, cwu, cbg, cbu, wd, l2g, l2b)]
    args = [h1, wug, wuu, bug, buu, cwg, cwu, cbg, cbu, wd, l2g, l2b]
    if has_state:
        in_specs += [wide] * 4
        args += list(state)
        u_spec, u_shape = wide, jax.ShapeDtypeStruct((R, W), F32)
    else:
        u_spec = pl.BlockSpec((1, SUBLANES, W), lambda i: (i, 0, 0))
        u_shape = jax.ShapeDtypeStruct((n_tiles, SUBLANES, W), F32)
    return pl.pallas_call(
        functools.partial(_ffn_kernel, seq_len=seq_len, has_state=has_state),
        grid=(n_tiles,),
        in_specs=in_specs,
        out_specs=[row, u_spec, u_spec],
        out_shape=[jax.ShapeDtypeStruct((R, D), F32), u_shape, u_shape],
        scratch_shapes=[pltpu.VMEM((SUBLANES, W), F32), pltpu.VMEM((SUBLANES, W), F32)],
        compiler_params=pltpu.CompilerParams(dimension_semantics=("arbitrary",), vmem_limit_bytes=VMEM_LIMIT),
        name="ffn_state" if has_state else "ffn",
    )(*args)


def _rope_tables(pos):
    inv_freq = 1.0 / (ROPE_THETA ** (jnp.arange(0, HEAD_DIM, 2, dtype=F32) / HEAD_DIM))
    ang = pos.astype(F32)[:, None] * inv_freq[None, :]
    ang = jnp.concatenate([ang, ang], axis=-1)
    sign = jnp.concatenate([-jnp.ones((HEAD_DIM // 2,), F32), jnp.ones((HEAD_DIM // 2,), F32)])
    return jnp.tile(jnp.cos(ang), (1, 2)), jnp.tile(jnp.sin(ang) * sign, (1, 2))


def _row_tile(L, target):
    best = SUBLANES
    for t in range(SUBLANES, target + 1, SUBLANES):
        if L % t == 0:
            best = t
    return best


def kernel(x_prompt, x_sample, cache_k, cache_v, state_gla, state_ffn_conv, page_table, meta_tokens, ln_in_g, ln_in_b, w_in, w_gate_up, b_gate, gla_norm_g, lambda_q1, lambda_k1, lambda_q2, lambda_k2, diff_norm_g, w_o, ln1_g, ln1_b, w_up, b_up, conv_w, conv_b, w_down, ln2_g, ln2_b):
    B, S, D = x_prompt.shape
    Bd, T, _ = x_sample.shape
    L = N_META + S
    n_phys, page_size = cache_k.shape[1], cache_k.shape[2]
    past = page_table.shape[1] * page_size
    d_ff = w_down.shape[1]
    l = 0

    wi = w_in[l]
    offs = [0, 512, 1024, 2048, 3072, 3088, 4112, 5136, 6160, 7184, 8208]
    seg = lambda i: wi[:, offs[i]:offs[i + 1]]
    w_main = jnp.concatenate([seg(0), seg(1), seg(2), seg(3), seg(5), seg(6), seg(7), seg(8), seg(9)],
                             axis=1).astype(BF16)
    w_ga = jnp.pad(seg(4), ((0, 0), (0, LANES - GLA_GATE_RANK))).astype(BF16)
    w_gu = jnp.pad(w_gate_up[l], ((0, LANES - GLA_GATE_RANK), (0, 0))).astype(BF16)
    lam = (jnp.exp(jnp.sum(lambda_q1[l] * lambda_k1[l])) - jnp.exp(jnp.sum(lambda_q2[l] * lambda_k2[l]))
           + LAM_INIT).reshape(1).astype(F32)
    r1 = lambda v: v.reshape(1, -1)
    vecs = [r1(ln_in_g), r1(ln_in_b), r1(gla_norm_g[l]), r1(diff_norm_g[l]), r1(ln1_g[l]), r1(ln1_b[l]),
            r1(ln2_g[l]), r1(ln2_b[l])]
    wo = w_o[l].astype(BF16)
    ffn_w = (w_up[l][:, :d_ff].astype(BF16), w_up[l][:, d_ff:].astype(BF16), b_up[l][None, :d_ff],
             b_up[l][None, d_ff:], conv_w[l][:, :d_ff], conv_w[l][:, d_ff:], conv_b[l][None, :d_ff],
             conv_b[l][None, d_ff:], w_down[l].astype(BF16), vecs[6], vecs[7])

    xp = jnp.concatenate([jnp.broadcast_to(meta_tokens[None], (B, N_META, D)), x_prompt], axis=1).reshape(B * L, D)
    tm = _row_tile(L, 344)
    cos_p, sin_p = _rope_tables(jnp.arange(L))
    gq, gk, la, gr, ma, mb, kf, vf, gv, qb, kb, vb = _inproj(xp, ln_in_g, ln_in_b, cos_p, sin_p, w_main, w_ga, w_gu,
                                                            b_gate[l], tm)
    b3 = lambda a: a.reshape(B, L, a.shape[-1])
    o_gla, s_p = _gla(b3(gq), b3(gk), b3(la), b3(gv), jnp.zeros((B, GLA_HEADS, GLA_DK, GLA_DV), F32))

    Rs = Bd * T
    xs = x_sample.reshape(Rs, D)
    cos_s, sin_s = _rope_tables(past + jnp.arange(T))
    cos_s, sin_s = jnp.tile(cos_s, (Bd, 1)), jnp.tile(sin_s, (Bd, 1))
    sq, sk, sa, sr, na, nb, ekf, evf, sv, eqb, ekb, evb = _inproj(xs, ln_in_g, ln_in_b, cos_s, sin_s, w_main, w_ga,
                                                                  w_gu, b_gate[l], Rs)
    q4 = eqb.reshape(Bd, T, 2 * DIFF_HEADS, HEAD_DIM)
    eye = jnp.eye(2 * DIFF_HEADS, dtype=BF16)
    q_rows = (q4.transpose(0, 2, 1, 3)[:, :, :, None, :] * eye[None, :, None, :, None]).reshape(
        Bd, 2 * DIFF_HEADS * T, D)
    Tk = 2 * SUBLANES
    padK = lambda a: jnp.pad(a.reshape(Bd, T, D), ((0, 0), (0, Tk - T), (0, 0)))
    cache_kt = cache_k[l].transpose(0, 2, 3, 1).reshape(n_phys, D, page_size)
    cache_vr = cache_v[l].reshape(n_phys, page_size * DIFF_HEADS, 2 * HEAD_DIM)
    o_diff, o_diff_s = _diff_attn(page_table, lam, b3(qb), b3(kb), b3(vb), q_rows, padK(ekb), padK(evb),
                                  cache_kt, cache_vr, T)

    h1 = _merge(xp, o_gla.reshape(B * L, D), gr, o_diff.reshape(B * L, D), ma, mb, vecs, wo, tm)
    y_full, utg, utu = _ffn(h1, *ffn_w, tm=tm, seq_len=L)
    y_prompt = y_full.reshape(B, L, D)[:, N_META:]
    tiles_per_seq = L // tm
    last = lambda u: u.reshape(B, tiles_per_seq, SUBLANES, d_ff)[:, -1, SUBLANES - (CONV_WIDTH - 1):, :]
    conv_p = jnp.concatenate([last(utg), last(utu)], axis=-1)
    k_prompt = kf.reshape(1, B, L, 2 * DIFF_HEADS, HEAD_DIM)
    v_prompt = vf.reshape(1, B, L, DIFF_HEADS, 2 * HEAD_DIM)

    Tp = GLA_CHUNK
    padT = lambda a: jnp.pad(a.reshape(Bd, T, a.shape[-1]), ((0, 0), (Tp - T, 0), (0, 0)))
    o_gla_s, s_s = _gla(padT(sq), padT(sk), padT(sa), padT(sv), state_gla[l])
    o_gla_s = o_gla_s[:, Tp - T:].reshape(Rs, D)
    st = state_ffn_conv[l]
    zero = jnp.zeros((Bd, 1, 2 * d_ff), F32)
    prev1 = jnp.concatenate([st[:, 1:2]] + [zero] * (T - 1), axis=1).reshape(Rs, 2 * d_ff)
    prev2 = jnp.concatenate([st[:, 0:1], st[:, 1:2]] + [zero] * (T - 2), axis=1).reshape(Rs, 2 * d_ff)
    state = (prev1[:, :d_ff], prev1[:, d_ff:], prev2[:, :d_ff], prev2[:, d_ff:])
    h1_s = _merge(xs, o_gla_s, sr, o_diff_s.reshape(Rs, D), na, nb, vecs, wo, Rs)
    y_s, usg, usu = _ffn(h1_s, *ffn_w, tm=Rs, seq_len=T, state=state)
    y_sample = y_s.reshape(Bd, T, D)
    new_rows = lambda u: u.reshape(Bd, T, d_ff)[:, T - (CONV_WIDTH - 1):]
    conv_s = jnp.concatenate([new_rows(usg), new_rows(usu)], axis=-1)
    k_sample = ekf.reshape(1, Bd, T, 2 * DIFF_HEADS, HEAD_DIM)
    v_sample = evf.reshape(1, Bd, T, DIFF_HEADS, 2 * HEAD_DIM)

    return (y_prompt, y_sample, k_prompt, v_prompt, s_p[None], conv_p[None], k_sample, v_sample, s_s[None],
            conv_s[None])
```

```python
import functools
import math

import jax
import jax.numpy as jnp
from jax import lax
from jax.experimental import pallas as pl
from jax.experimental.pallas import tpu as pltpu

F32 = jnp.float32
BF16 = jnp.bfloat16

N_META = 16
GLA_HEADS = 4
GLA_DK = 128
GLA_DV = 256
GLA_GATE_RANK = 16
GLA_GATE_TAU = 16.0
HEAD_DIM = 64
DIFF_HEADS = 8
ROPE_THETA = 10000.0
NORM_EPS = 1e-5
CONV_WIDTH = 3
DEPTH = 1
DEEPNORM_ALPHA = (2.0 * DEPTH) ** 0.25
LAM_INIT = 0.8 - 0.6 * math.exp(-0.3 * 0)
LOG2E = math.log2(math.e)

LANES = 128
SUBLANES = 8
VMEM_LIMIT = 56 * 1024 * 1024
GLA_CHUNK = 128
GLA_DIAG = 8
GLA_HEADS_PER_STEP = 2
ATT_BLOCK = 128
ATT_GROUP = 4
PAGES_PER_STEP = 16


def _const_spec(shape):
    return pl.BlockSpec(shape, lambda *_: (0,) * len(shape), pipeline_mode=pl.Buffered(1))


def _layer_norm(x, g, b):
    mu = jnp.mean(x, axis=-1, keepdims=True)
    xc = x - mu
    var = jnp.mean(xc * xc, axis=-1, keepdims=True)
    return xc * lax.rsqrt(var + NORM_EPS) * g + b


def _sigmoid(x):
    return 1.0 / (1.0 + jnp.exp(-x))


def _dot(a, b):
    return jnp.dot(a, b, preferred_element_type=F32)


def _dot_nt(a, b):
    return lax.dot_general(a, b, (((1,), (1,)), ((), ())), preferred_element_type=F32)


def _inproj_kernel(x_ref, g_ref, b_ref, cos_ref, sin_ref, w_ref, wga_ref, wgu_ref, bg_ref,
                   gq_o, gk_o, la_o, gr_o, ma_o, mb_o, kf_o, vf_o, gv_o, q_o, k_o, v_o):
    h = _layer_norm(x_ref[...], g_ref[...], b_ref[...])
    hb = h.astype(BF16)

    def seg(c0, n):
        return _dot(hb, w_ref[:, c0:c0 + n])

    gq_o[...] = seg(0, 512) * (GLA_DK ** -0.5)
    gk_o[...] = seg(512, 512)
    gv_o[...] = seg(1024, 1024).astype(BF16)
    gr_o[...] = seg(2048, 1024)

    ga = _dot(hb, wga_ref[...])
    xg = _dot(ga.astype(BF16), wgu_ref[...]) + bg_ref[...]
    la_o[...] = (jnp.minimum(xg, 0.0) - jnp.log1p(jnp.exp(-jnp.abs(xg)))) * (1.0 / GLA_GATE_TAU)

    cos = cos_ref[...]
    sin = sin_ref[...]
    lane = lax.broadcasted_iota(jnp.int32, cos.shape, 1)
    first_half = (lane % HEAD_DIM) < (HEAD_DIM // 2)

    def rope(z):
        outs = []
        for g in range(z.shape[1] // LANES):
            zg = z[:, g * LANES:(g + 1) * LANES]
            rot = jnp.where(first_half, pltpu.roll(zg, LANES - HEAD_DIM // 2, 1), pltpu.roll(zg, HEAD_DIM // 2, 1))
            outs.append(zg * cos + rot * sin)
        return jnp.concatenate(outs, axis=1)

    q_o[...] = (rope(seg(3072, 1024)) * (HEAD_DIM ** -0.5 * LOG2E)).astype(BF16)
    kr = rope(seg(4096, 1024))
    kf_o[...] = kr
    k_o[...] = kr.astype(BF16)
    vv = seg(5120, 1024)
    vf_o[...] = vv
    v_o[...] = vv.astype(BF16)
    ma_o[...] = seg(6144, 1024)
    mb_o[...] = seg(7168, 1024)


def _inproj(x, ln_g, ln_b, cos_t, sin_t, w_main, w_ga, w_gu, b_gate, tm):
    R, D = x.shape
    period = cos_t.shape[0] // tm
    row = lambda w: pl.BlockSpec((tm, w), lambda i: (i, 0))
    tab = pl.BlockSpec((tm, LANES), lambda i: (i % period, 0))
    f32_out = lambda w: jax.ShapeDtypeStruct((R, w), F32)
    bf_out = lambda w: jax.ShapeDtypeStruct((R, w), BF16)
    return pl.pallas_call(
        _inproj_kernel,
        grid=(R // tm,),
        in_specs=[row(D), _const_spec((1, D)), _const_spec((1, D)), tab, tab,
                  _const_spec(w_main.shape), _const_spec(w_ga.shape), _const_spec(w_gu.shape),
                  _const_spec((1, 512))],
        out_specs=[row(512), row(512), row(512), row(1024), row(1024), row(1024), row(1024), row(1024),
                   row(1024), row(1024), row(1024), row(1024)],
        out_shape=[f32_out(512), f32_out(512), f32_out(512), f32_out(1024), f32_out(1024), f32_out(1024),
                   f32_out(1024), f32_out(1024), bf_out(1024), bf_out(1024), bf_out(1024), bf_out(1024)],
        compiler_params=pltpu.CompilerParams(dimension_semantics=("parallel",), vmem_limit_bytes=VMEM_LIMIT),
        name="inproj",
    )(x, ln_g.reshape(1, D), ln_b.reshape(1, D), cos_t, sin_t, w_main, w_ga, w_gu, b_gate.reshape(1, 512))


def _split3(x):
    hi = x.astype(BF16)
    r1 = x - hi.astype(F32)
    mid = r1.astype(BF16)
    lo = (r1 - mid.astype(F32)).astype(BF16)
    return hi, mid, lo


def _gla_consts(C):
    row = lax.broadcasted_iota(jnp.int32, (C, C), 0)
    col = lax.broadcasted_iota(jnp.int32, (C, C), 1)
    rowk = lax.broadcasted_iota(jnp.int32, (C, GLA_DK), 0)
    tri = jnp.where(row >= col, 1.0, 0.0).astype(BF16)
    levels = []
    hs = C // 2
    while hs >= GLA_DIAG:
        grp = 2 * hs
        upper = (rowk % grp) >= hs
        keep = ((row // grp) == (col // grp)) & ((row % grp) >= hs) & ((col % grp) < hs)
        levels.append((hs, jnp.where(upper, 0.0, -jnp.inf), jnp.where(upper, -jnp.inf, 0.0),
                       jnp.where(keep, 1.0, 0.0)))
        hs //= 2
    d = GLA_DIAG
    diag = [(jnp.where((rowk % d) >= jj, 0.0, -jnp.inf), jnp.where(col == (row // d) * d + jj, 1.0, 0.0))
            for jj in range(d)]
    return tri, levels, diag


def _gla_intra(q, k, bc, C, levels, diag):
    att = jnp.zeros((C, C), F32)
    for hs, q_bias, k_bias, keep in levels:
        grp = 2 * hs
        refs = jnp.concatenate(
            [jnp.broadcast_to(bc[g * grp + hs - 1:g * grp + hs, :], (grp, GLA_DK)) for g in range(C // grp)], axis=0)
        qs = (q * jnp.exp2((bc - refs) + q_bias)).astype(BF16)
        ks = (k * jnp.exp2((refs - bc) + k_bias)).astype(BF16)
        att = att + _dot_nt(qs, ks) * keep
    d = GLA_DIAG
    nb = C // d
    k3 = k.reshape(nb, d, GLA_DK)
    b3 = bc.reshape(nb, d, GLA_DK)
    for jj, (bias, place) in enumerate(diag):
        kb = jnp.broadcast_to(k3[:, jj:jj + 1, :], (nb, d, GLA_DK)).reshape(C, GLA_DK)
        bb = jnp.broadcast_to(b3[:, jj:jj + 1, :], (nb, d, GLA_DK)).reshape(C, GLA_DK)
        colsum = jnp.sum(q * kb * jnp.exp2((bc - bb) + bias), axis=1, keepdims=True)
        att = att + colsum * place
    return att


def _gla_chunk(q, k, la, v, S, C, consts):
    tri, levels, diag = consts
    hi, mid, lo = _split3(la * LOG2E)
    bc = _dot(tri, hi) + _dot(tri, mid) + _dot(tri, lo)
    o = _dot((q * jnp.exp2(bc)).astype(BF16), S.astype(BF16))
    att = _gla_intra(q, k, bc, C, levels, diag)
    o = o + _dot(att.astype(BF16), v)
    bct = bc.T
    blt = bct[:, C - 1:C]
    kdt = (k.T * jnp.exp2(blt - bct)).astype(BF16)
    S_new = jnp.exp2(blt) * S + _dot(kdt, v)
    return o, S_new


def _gla_kernel(q_ref, k_ref, la_ref, v_ref, s0_ref, o_ref, s_ref, s_scr, *, n_full, tail):
    C = GLA_CHUNK
    consts = _gla_consts(C)
    s_scr[...] = s0_ref[0]

    def do_chunk(r0, skip):
        sl = pl.ds(r0, C)
        for h in range(GLA_HEADS_PER_STEP):
            kc = slice(h * GLA_DK, (h + 1) * GLA_DK)
            vc = slice(h * GLA_DV, (h + 1) * GLA_DV)
            q, k, la, v = q_ref[0, sl, kc], k_ref[0, sl, kc], la_ref[0, sl, kc], v_ref[0, sl, vc]
            if skip:
                live = lax.broadcasted_iota(jnp.int32, (C, GLA_DK), 0) >= skip
                live_v = lax.broadcasted_iota(jnp.int32, (C, GLA_DV), 0) >= skip
                q, k, la = jnp.where(live, q, 0.0), jnp.where(live, k, 0.0), jnp.where(live, la, 0.0)
                v = jnp.where(live_v, v, jnp.zeros_like(v))
            o, S_new = _gla_chunk(q, k, la, v, s_scr[h], C, consts)
            if skip:
                o_ref[0, pl.ds(r0 + skip, C - skip), vc] = o[skip:]
            else:
                o_ref[0, sl, vc] = o
            s_scr[h] = S_new

    def body(i, carry):
        do_chunk(pl.multiple_of(i * C, C), 0)
        return carry

    lax.fori_loop(0, n_full, body, 0, unroll=2)
    if tail:
        do_chunk((n_full - 1) * C + tail, C - tail)
    s_ref[0] = s_scr[...]


def _gla(q, k, la, v, s0):
    B, L, _ = q.shape
    n_full, tail = divmod(L, GLA_CHUNK)
    HP = GLA_HEADS_PER_STEP
    qk = pl.BlockSpec((1, L, HP * GLA_DK), lambda b, h: (b, 0, h))
    vv = pl.BlockSpec((1, L, HP * GLA_DV), lambda b, h: (b, 0, h))
    st = pl.BlockSpec((1, HP, GLA_DK, GLA_DV), lambda b, h: (b, h, 0, 0))
    return pl.pallas_call(
        functools.partial(_gla_kernel, n_full=n_full, tail=tail),
        grid=(B, GLA_HEADS // HP),
        in_specs=[qk, qk, qk, vv, st],
        out_specs=[vv, st],
        out_shape=[jax.ShapeDtypeStruct((B, L, GLA_HEADS * GLA_DV), F32),
                   jax.ShapeDtypeStruct((B, GLA_HEADS, GLA_DK, GLA_DV), F32)],
        scratch_shapes=[pltpu.VMEM((HP, GLA_DK, GLA_DV), F32)],
        compiler_params=pltpu.CompilerParams(dimension_semantics=("parallel", "parallel"),
                                             vmem_limit_bytes=VMEM_LIMIT),
        name="gla",
    )(q, k, la, v, s0)


def _stack_maps(q):
    lane = lax.broadcasted_iota(jnp.int32, q.shape, 1)
    zero = jnp.zeros_like(q)
    return jnp.concatenate([jnp.where(lane < HEAD_DIM, q, zero), jnp.where(lane >= HEAD_DIM, q, zero)], axis=0)


def _split_blocks(n_blocks, parts):
    total = n_blocks * (n_blocks + 1) // 2
    bounds, acc = [0], 0
    for i in range(n_blocks):
        acc += i + 1
        if len(bounds) < parts and acc * parts >= total * len(bounds) and n_blocks - (i + 1) >= parts - len(bounds):
            bounds.append(i + 1)
    bounds.append(n_blocks)
    return [(bounds[p], bounds[p + 1]) for p in range(parts)]


def _diff_attn_kernel(pt_ref, lam_ref, q_ref, k_ref, v_ref, qs_ref, kn_ref, vn_ref, *refs, L, T, parts,
                      steps_per_seq):
    G = PAGES_PER_STEP
    k_refs, v_refs = refs[:G], refs[G:2 * G]
    o_ref, os_ref, k_scr, v_scr, m_scr, l_scr, acc_scr = refs[2 * G:]
    lam = lam_ref[0]
    C = ATT_BLOCK
    Lpad = k_scr.shape[0]
    j = pl.program_id(2)
    step = (pl.program_id(0) * pl.num_programs(1) + pl.program_id(1)) * pl.num_programs(2) + j
    seq_step = step % steps_per_seq
    qs = qs_ref[0]
    rows_per_head = 2 * T
    page = k_refs[0].shape[2]

    @pl.when(j == 0)
    def _():
        k_scr[0:L, :] = k_ref[0]
        v_scr[0:L, 0:LANES] = v_ref[0]
        if Lpad > L:
            k_scr[L:Lpad, :] = jnp.zeros((Lpad - L, LANES), BF16)
            v_scr[L:Lpad, 0:LANES] = jnp.zeros((Lpad - L, LANES), BF16)
        v_scr[:, LANES:2 * LANES] = jnp.ones((Lpad, LANES), BF16)

    @pl.when(seq_step == 0)
    def _():
        s = _dot_nt(qs, kn_ref[0])
        t_row = lax.broadcasted_iota(jnp.int32, s.shape, 0) % T
        t_col = lax.broadcasted_iota(jnp.int32, s.shape, 1)
        s = jnp.where(t_col <= t_row, s, -jnp.inf)
        m = jnp.max(s, axis=1, keepdims=True)
        p = jnp.exp2(s - m)
        m_scr[...] = m
        l_scr[...] = jnp.sum(p, axis=1, keepdims=True)
        vn = vn_ref[0]
        acc_scr[...] = jnp.concatenate(
            [_dot(p[h * rows_per_head:(h + 1) * rows_per_head].astype(BF16), vn[:, h * LANES:(h + 1) * LANES])
             for h in range(DIFF_HEADS)], axis=0)

    def sample_scores():
        s = jnp.concatenate([_dot(qs, k_refs[g][0].astype(BF16)) for g in range(G)], axis=1)
        m_old = m_scr[...]
        m_new = jnp.maximum(m_old, jnp.max(s, axis=1, keepdims=True))
        return jnp.exp2(s - m_new), jnp.exp2(m_old - m_new), m_new

    def sample_values(p, corr, m_new):
        pv = []
        for h in range(DIFF_HEADS):
            ph = p[h * rows_per_head:(h + 1) * rows_per_head].astype(BF16)
            vh = jnp.concatenate(
                [v_refs[g][0, pl.ds(h, page, stride=DIFF_HEADS), :].astype(BF16) for g in range(G)], axis=0)
            pv.append(_dot(ph, vh))
        m_scr[...] = m_new
        l_scr[...] = l_scr[...] * corr + jnp.sum(p, axis=1, keepdims=True)
        acc_scr[...] = acc_scr[...] * corr + jnp.concatenate(pv, axis=0)

    def q_block(r0, Lk, band):
        q2 = _stack_maps(q_ref[0, pl.ds(r0, C), :])
        s = _dot_nt(q2, k_scr[0:Lk, :])
        sb = s[:, Lk - band:]
        row = r0 + lax.broadcasted_iota(jnp.int32, sb.shape, 0) % C
        col = (Lk - band) + lax.broadcasted_iota(jnp.int32, sb.shape, 1)
        sb = jnp.where(col <= row, sb, -jnp.inf)
        m = jnp.max(sb, axis=1, keepdims=True)
        if Lk > band:
            sa = s[:, :Lk - band]
            m = jnp.maximum(m, jnp.max(sa, axis=1, keepdims=True))
            p = jnp.concatenate([jnp.exp2(sa - m).astype(BF16), jnp.exp2(sb - m).astype(BF16)], axis=1)
        else:
            p = jnp.exp2(sb - m).astype(BF16)
        ov = _dot(p, v_scr[0:Lk, :])
        o = ov[:, :LANES] / ov[:, LANES:]
        o_ref[0, pl.ds(r0, C), :] = o[:C] - lam * o[C:]

    for part, (b0, b1) in enumerate(parts):
        @pl.when(j == part)
        def _(b0=b0, b1=b1):
            page_state = sample_scores()
            for i0 in range(b0, b1, ATT_GROUP):
                i1 = min(i0 + ATT_GROUP, b1)
                first_row = (min(i0 * C, L - C) // C) * C
                for i in range(i0, i1):
                    q_block(min(i * C, L - C), i1 * C, i1 * C - first_row)
            sample_values(*page_state)

    @pl.when(seq_step == steps_per_seq - 1)
    def _():
        o = acc_scr[...] / l_scr[...]
        pieces = []
        for h in range(DIFF_HEADS):
            tile = o[h * rows_per_head:(h + 1) * rows_per_head]
            pieces.append(tile[:T] - lam * tile[T:])
        os_ref[0] = jnp.concatenate(pieces, axis=1)


def _diff_attn(page_table, lam, q, k, v, q_rows, k_new, v_new, cache_kt, cache_v, T):
    B, L, W = q.shape
    Bd, R, _ = q_rows.shape
    n_pages = page_table.shape[1]
    G = PAGES_PER_STEP
    page = cache_kt.shape[2]
    Tp = k_new.shape[1]
    Lpad = -(-L // ATT_BLOCK) * ATT_BLOCK
    steps_per_seq = n_pages // G
    J = (Bd * steps_per_seq) // (B * DIFF_HEADS)
    assert J * B * DIFF_HEADS == Bd * steps_per_seq and n_pages % G == 0 and 1 <= J <= Lpad // ATT_BLOCK
    parts = _split_blocks(Lpad // ATT_BLOCK, J)

    def seq_of(b, h, j):
        return ((b * DIFF_HEADS + h) * J + j) // steps_per_seq

    def page_of(b, h, j, pt, g):
        step = (b * DIFF_HEADS + h) * J + j
        return pt[step // steps_per_seq, (step % steps_per_seq) * G + g]

    blk = pl.BlockSpec((1, L, LANES), lambda b, h, j, pt, lam_ref: (b, 0, h))
    per_seq = lambda shape: pl.BlockSpec((1,) + shape, lambda b, h, j, pt, lam_ref: (seq_of(b, h, j), 0, 0))
    kpage = lambda g: pl.BlockSpec((1, W, page), lambda b, h, j, pt, lam_ref, g=g: (page_of(b, h, j, pt, g), 0, 0))
    vpage = lambda g: pl.BlockSpec((1, page * DIFF_HEADS, LANES),
                                   lambda b, h, j, pt, lam_ref, g=g: (page_of(b, h, j, pt, g), 0, 0))
    return pl.pallas_call(
        functools.partial(_diff_attn_kernel, L=L, T=T, parts=parts, steps_per_seq=steps_per_seq),
        grid_spec=pltpu.PrefetchScalarGridSpec(
            num_scalar_prefetch=2, grid=(B, DIFF_HEADS, J),
            in_specs=[blk, blk, blk, per_seq((R, W)), per_seq((Tp, W)), per_seq((Tp, W))]
                     + [kpage(g) for g in range(G)] + [vpage(g) for g in range(G)],
            out_specs=[blk, per_seq((T, W))],
            scratch_shapes=[pltpu.VMEM((Lpad, LANES), BF16), pltpu.VMEM((Lpad, 2 * LANES), BF16),
                            pltpu.VMEM((R, 1), F32), pltpu.VMEM((R, 1), F32), pltpu.VMEM((R, LANES), F32)]),
        out_shape=[jax.ShapeDtypeStruct((B, L, W), F32), jax.ShapeDtypeStruct((Bd, T, W), F32)],
        compiler_params=pltpu.CompilerParams(dimension_semantics=("arbitrary", "arbitrary", "arbitrary"),
                                             vmem_limit_bytes=VMEM_LIMIT),
        name="diff_attn",
    )(page_table, lam, q, k, v, q_rows, k_new, v_new, *([cache_kt] * G), *([cache_v] * G))


def _head_rms(x, g, width):
    outs = []
    for h in range(x.shape[1] // width):
        xh = x[:, h * width:(h + 1) * width]
        outs.append(xh * lax.rsqrt(jnp.mean(xh * xh, axis=1, keepdims=True) + NORM_EPS) * g)
    return jnp.concatenate(outs, axis=1)


def _merge_kernel(x_ref, og_ref, gr_ref, od_ref, ma_ref, mb_ref, lng_ref, lnb_ref, gg_ref, dg_ref, wo_ref,
                  l1g_ref, l1b_ref, h1_ref):
    h = _layer_norm(x_ref[...], lng_ref[...], lnb_ref[...])
    a = _head_rms(og_ref[...], gg_ref[...], GLA_DV)
    a = a * (gr_ref[...] * _sigmoid(gr_ref[...]))
    d = _head_rms(od_ref[...], dg_ref[...], 2 * HEAD_DIM) * (1.0 - LAM_INIT)
    mixed = _sigmoid(ma_ref[...]) * a + _sigmoid(mb_ref[...]) * d
    h1_ref[...] = _layer_norm(DEEPNORM_ALPHA * h + _dot(mixed.astype(BF16), wo_ref[...]), l1g_ref[...],
                              l1b_ref[...])


def _merge(x, og, gr, od, ma, mb, vecs, wo, tm):
    R, D = x.shape
    row = pl.BlockSpec((tm, D), lambda i: (i, 0))
    return pl.pallas_call(
        _merge_kernel,
        grid=(R // tm,),
        in_specs=[row] * 6 + [_const_spec((1, D)), _const_spec((1, D)), _const_spec((1, GLA_DV)),
                              _const_spec((1, 2 * HEAD_DIM)), _const_spec(wo.shape), _const_spec((1, D)),
                              _const_spec((1, D))],
        out_specs=row,
        out_shape=jax.ShapeDtypeStruct((R, D), F32),
        compiler_params=pltpu.CompilerParams(dimension_semantics=("parallel",), vmem_limit_bytes=VMEM_LIMIT),
        name="merge",
    )(x, og, gr, od, ma, mb, *vecs[:4], wo, *vecs[4:6])


def _ffn_kernel(*refs, seq_len, has_state):
    (h1_ref, wug_ref, wuu_ref, bug_ref, buu_ref, cwg_ref, cwu_ref, cbg_ref, cbu_ref, wd_ref, l2g_ref,
     l2b_ref) = refs[:12]
    if has_state:
        p1g_ref, p1u_ref, p2g_ref, p2u_ref, y_ref, ug_ref, uu_ref, cg_scr, cu_scr = refs[12:]
    else:
        y_ref, ug_ref, uu_ref, cg_scr, cu_scr = refs[12:]
    tm = h1_ref.shape[0]
    W = wug_ref.shape[1]

    @pl.when(pl.program_id(0) == 0)
    def _():
        cg_scr[...] = jnp.zeros_like(cg_scr)
        cu_scr[...] = jnp.zeros_like(cu_scr)

    h1 = h1_ref[...]
    h1b = h1.astype(BF16)
    pos = (pl.program_id(0) * tm + lax.broadcasted_iota(jnp.int32, (tm, W), 0)) % seq_len
    row8 = lax.broadcasted_iota(jnp.int32, (SUBLANES, W), 0)

    def shifted(u, s, prev8, state):
        r = pltpu.roll(u, s, 0)
        head = jnp.where(row8 < s, pltpu.roll(prev8, s, 0), r[:SUBLANES])
        r = jnp.concatenate([head, r[SUBLANES:]], axis=0)
        return jnp.where(pos >= s, r, state)

    def conv(u, cw, cb, prev8, st1, st2):
        return cb + cw[0:1] * shifted(u, 2, prev8, st2) + cw[1:2] * shifted(u, 1, prev8, st1) + cw[2:3] * u

    u_g = _dot(h1b, wug_ref[...]) + bug_ref[...]
    u_u = _dot(h1b, wuu_ref[...]) + buu_ref[...]
    if has_state:
        st = (p1g_ref[...], p2g_ref[...], p1u_ref[...], p2u_ref[...])
        ug_ref[...] = u_g
        uu_ref[...] = u_u
    else:
        st = (0.0, 0.0, 0.0, 0.0)
        ug_ref[0] = u_g[tm - SUBLANES:]
        uu_ref[0] = u_u[tm - SUBLANES:]
    gate = conv(u_g, cwg_ref[...], cbg_ref[...], cg_scr[...], st[0], st[1])
    up = conv(u_u, cwu_ref[...], cbu_ref[...], cu_scr[...], st[2], st[3])
    cg_scr[...] = u_g[tm - SUBLANES:]
    cu_scr[...] = u_u[tm - SUBLANES:]
    gelu = 0.5 * gate * (1.0 + jnp.tanh(math.sqrt(2.0 / math.pi) * (gate + 0.044715 * (gate * gate * gate))))
    f = _dot((gelu * up).astype(BF16), wd_ref[...])
    y_ref[...] = _layer_norm(DEEPNORM_ALPHA * h1 + f, l2g_ref[...], l2b_ref[...])


def _ffn(h1, wug, wuu, bug, buu, cwg, cwu, cbg, cbu, wd, l2g, l2b, tm, seq_len, state=None):
    R, D = h1.shape
    W = wug.shape[1]
    n_tiles = R // tm
    has_state = state is not None
    row = pl.BlockSpec((tm, D), lambda i: (i, 0))
    wide = pl.BlockSpec((tm, W), lambda i: (i, 0))
    in_specs = [row] + [_const_spec(a.shape) for a in (wug, wuu, bug, buu, cwg, cwu, cbg, cbu, wd, l2g, l2b)]
    args = [h1, wug, wuu, bug, buu, cwg, cwu, cbg, cbu, wd, l2g, l2b]
    if has_state:
        in_specs += [wide] * 4
        args += list(state)
        u_spec, u_shape = wide, jax.ShapeDtypeStruct((R, W), F32)
    else:
        u_spec = pl.BlockSpec((1, SUBLANES, W), lambda i: (i, 0, 0))
        u_shape = jax.ShapeDtypeStruct((n_tiles, SUBLANES, W), F32)
    return pl.pallas_call(
        functools.partial(_ffn_kernel, seq_len=seq_len, has_state=has_state),
        grid=(n_tiles,),
        in_specs=in_specs,
        out_specs=[row, u_spec, u_spec],
        out_shape=[jax.ShapeDtypeStruct((R, D), F32), u_shape, u_shape],
        scratch_shapes=[pltpu.VMEM((SUBLANES, W), F32), pltpu.VMEM((SUBLANES, W), F32)],
        compiler_params=pltpu.CompilerParams(dimension_semantics=("arbitrary",), vmem_limit_bytes=VMEM_LIMIT),
        name="ffn_state" if has_state else "ffn",
    )(*args)


def _rope_tables(pos):
    inv_freq = 1.0 / (ROPE_THETA ** (jnp.arange(0, HEAD_DIM, 2, dtype=F32) / HEAD_DIM))
    ang = pos.astype(F32)[:, None] * inv_freq[None, :]
    ang = jnp.concatenate([ang, ang], axis=-1)
    sign = jnp.concatenate([-jnp.ones((HEAD_DIM // 2,), F32), jnp.ones((HEAD_DIM // 2,), F32)])
    return jnp.tile(jnp.cos(ang), (1, 2)), jnp.tile(jnp.sin(ang) * sign, (1, 2))


def _row_tile(L, target):
    best = SUBLANES
    for t in range(SUBLANES, target + 1, SUBLANES):
        if L % t == 0:
            best = t
    return best


def kernel(x_prompt, x_sample, cache_k, cache_v, state_gla, state_ffn_conv, page_table, meta_tokens, ln_in_g, ln_in_b, w_in, w_gate_up, b_gate, gla_norm_g, lambda_q1, lambda_k1, lambda_q2, lambda_k2, diff_norm_g, w_o, ln1_g, ln1_b, w_up, b_up, conv_w, conv_b, w_down, ln2_g, ln2_b):
    B, S, D = x_prompt.shape
    Bd, T, _ = x_sample.shape
    L = N_META + S
    n_phys, page_size = cache_k.shape[1], cache_k.shape[2]
    past = page_table.shape[1] * page_size
    d_ff = w_down.shape[1]
    l = 0

    wi = w_in[l]
    offs = [0, 512, 1024, 2048, 3072, 3088, 4112, 5136, 6160, 7184, 8208]
    seg = lambda i: wi[:, offs[i]:offs[i + 1]]
    w_main = jnp.concatenate([seg(0), seg(1), seg(2), seg(3), seg(5), seg(6), seg(7), seg(8), seg(9)],
                             axis=1).astype(BF16)
    w_ga = jnp.pad(seg(4), ((0, 0), (0, LANES - GLA_GATE_RANK))).astype(BF16)
    w_gu = jnp.pad(w_gate_up[l], ((0, LANES - GLA_GATE_RANK), (0, 0))).astype(BF16)
    lam = (jnp.exp(jnp.sum(lambda_q1[l] * lambda_k1[l])) - jnp.exp(jnp.sum(lambda_q2[l] * lambda_k2[l]))
           + LAM_INIT).reshape(1).astype(F32)
    r1 = lambda v: v.reshape(1, -1)
    vecs = [r1(ln_in_g), r1(ln_in_b), r1(gla_norm_g[l]), r1(diff_norm_g[l]), r1(ln1_g[l]), r1(ln1_b[l]),
            r1(ln2_g[l]), r1(ln2_b[l])]
    wo = w_o[l].astype(BF16)
    ffn_w = (w_up[l][:, :d_ff].astype(BF16), w_up[l][:, d_ff:].astype(BF16), b_up[l][None, :d_ff],
             b_up[l][None, d_ff:], conv_w[l][:, :d_ff], conv_w[l][:, d_ff:], conv_b[l][None, :d_ff],
             conv_b[l][None, d_ff:], w_down[l].astype(BF16), vecs[6], vecs[7])

    xp = jnp.concatenate([jnp.broadcast_to(meta_tokens[None], (B, N_META, D)), x_prompt], axis=1).reshape(B * L, D)
    tm = _row_tile(L, 344)
    cos_p, sin_p = _rope_tables(jnp.arange(L))
    gq, gk, la, gr, ma, mb, kf, vf, gv, qb, kb, vb = _inproj(xp, ln_in_g, ln_in_b, cos_p, sin_p, w_main, w_ga, w_gu,
                                                            b_gate[l], tm)
    b3 = lambda a: a.reshape(B, L, a.shape[-1])
    o_gla, s_p = _gla(b3(gq), b3(gk), b3(la), b3(gv), jnp.zeros((B, GLA_HEADS, GLA_DK, GLA_DV), F32))

    Rs = Bd * T
    xs = x_sample.reshape(Rs, D)
    cos_s, sin_s = _rope_tables(past + jnp.arange(T))
    cos_s, sin_s = jnp.tile(cos_s, (Bd, 1)), jnp.tile(sin_s, (Bd, 1))
    sq, sk, sa, sr, na, nb, ekf, evf, sv, eqb, ekb, evb = _inproj(xs, ln_in_g, ln_in_b, cos_s, sin_s, w_main, w_ga,
                                                                  w_gu, b_gate[l], Rs)
    q4 = eqb.reshape(Bd, T, 2 * DIFF_HEADS, HEAD_DIM)
    eye = jnp.eye(2 * DIFF_HEADS, dtype=BF16)
    q_rows = (q4.transpose(0, 2, 1, 3)[:, :, :, None, :] * eye[None, :, None, :, None]).reshape(
        Bd, 2 * DIFF_HEADS * T, D)
    Tk = 2 * SUBLANES
    padK = lambda a: jnp.pad(a.reshape(Bd, T, D), ((0, 0), (0, Tk - T), (0, 0)))
    cache_kt = cache_k[l].transpose(0, 2, 3, 1).reshape(n_phys, D, page_size)
    cache_vr = cache_v[l].reshape(n_phys, page_size * DIFF_HEADS, 2 * HEAD_DIM)
    o_diff, o_diff_s = _diff_attn(page_table, lam, b3(qb), b3(kb), b3(vb), q_rows, padK(ekb), padK(evb),
                                  cache_kt, cache_vr, T)

    h1 = _merge(xp, o_gla.reshape(B * L, D), gr, o_diff.reshape(B * L, D), ma, mb, vecs, wo, tm)
    y_full, utg, utu = _ffn(h1, *ffn_w, tm=tm, seq_len=L)
    y_prompt = y_full.reshape(B, L, D)[:, N_META:]
    tiles_per_seq = L // tm
    last = lambda u: u.reshape(B, tiles_per_seq, SUBLANES, d_ff)[:, -1, SUBLANES - (CONV_WIDTH - 1):, :]
    conv_p = jnp.concatenate([last(utg), last(utu)], axis=-1)
    k_prompt = kf.reshape(1, B, L, 2 * DIFF_HEADS, HEAD_DIM)
    v_prompt = vf.reshape(1, B, L, DIFF_HEADS, 2 * HEAD_DIM)

    Tp = GLA_CHUNK
    padT = lambda a: jnp.pad(a.reshape(Bd, T, a.shape[-1]), ((0, 0), (Tp - T, 0), (0, 0)))
    o_gla_s, s_s = _gla(padT(sq), padT(sk), padT(sa), padT(sv), state_gla[l])
    o_gla_s = o_gla_s[:, Tp - T:].reshape(Rs, D)
    st = state_ffn_conv[l]
    zero = jnp.zeros((Bd, 1, 2 * d_ff), F32)
    prev1 = jnp.concatenate([st[:, 1:2]] + [zero] * (T - 1), axis=1).reshape(Rs, 2 * d_ff)
    prev2 = jnp.concatenate([st[:, 0:1], st[:, 1:2]] + [zero] * (T - 2), axis=1).reshape(Rs, 2 * d_ff)
    state = (prev1[:, :d_ff], prev1[:, d_ff:], prev2[:, :d_ff], prev2[:, d_ff:])
    h1_s = _merge(xs, o_gla_s, sr, o_diff_s.reshape(Rs, D), na, nb, vecs, wo, Rs)
    y_s, usg, usu = _ffn(h1_s, *ffn_w, tm=Rs, seq_len=T, state=state)
    y_sample = y_s.reshape(Bd, T, D)
    new_rows = lambda u: u.reshape(Bd, T, d_ff)[:, T - (CONV_WIDTH - 1):]
    conv_s = jnp.concatenate([new_rows(usg), new_rows(usu)], axis=-1)
    k_sample = ekf.reshape(1, Bd, T, 2 * DIFF_HEADS, HEAD_DIM)
    v_sample = evf.reshape(1, Bd, T, DIFF_HEADS, 2 * HEAD_DIM)

    return (y_prompt, y_sample, k_prompt, v_prompt, s_p[None], conv_p[None], k_sample, v_sample, s_s[None],
            conv_s[None])
```
